```python
import math
import jax, jax.numpy as jnp
from jax import lax
import numpy as np

D_MODEL = 2048
BATCH = 2
SEQ = 8192
DEPTH = 1

CHUNK = 64
GMLP_BLOCK = 128
Q_BLOCK = 128
MIX_WIDTH = D_MODEL
GMLP_WIDTH = MIX_WIDTH // 2
GMLP_GROUPS = 8
GMLP_GROUP_DIM = GMLP_WIDTH // GMLP_GROUPS
DIFF_WIDTH = MIX_WIDTH - GMLP_WIDTH
DIFF_HEADS = 8
DIFF_V_DIM = DIFF_WIDTH // DIFF_HEADS
DIFF_QK_DIM = DIFF_V_DIM // 2
IN_COLS = 2 * GMLP_WIDTH + 2 * DIFF_HEADS * 2 * DIFF_QK_DIM + DIFF_WIDTH
D_FF = ((8 * D_MODEL // 3 + 255) // 256) * 256
N_MOD = 9
ROPE_THETA = 10000.0
LN_EPS = 1e-5
ALPHA = (2 * DEPTH) ** 0.25
BETA = (8 * DEPTH) ** -0.25
NEG_INF = -1e30

kernel_name = "hybrid_gmlp_diffattn_macaron_deepnorm_adaln"


def layer_norm(x, g, b):
    xf = x.astype(jnp.float32)
    mu = jnp.mean(xf, axis=-1, keepdims=True)
    var = jnp.mean(jnp.square(xf - mu), axis=-1, keepdims=True)
    y = (xf - mu) * lax.rsqrt(var + LN_EPS) * g.astype(jnp.float32) + b.astype(jnp.float32)
    return y.astype(x.dtype)


def rms_norm(x, g):
    xf = x.astype(jnp.float32)
    y = xf * lax.rsqrt(jnp.mean(jnp.square(xf), axis=-1, keepdims=True) + LN_EPS)
    return (y * g.astype(jnp.float32)).astype(x.dtype)


def modulate(x, shift, scale):
    return x * (1 + scale[:, None, :]) + shift[:, None, :]


def swiglu_ffn(h, w_gu, w_down):
    gate, up = jnp.split(h @ w_gu, 2, axis=-1)
    return (jax.nn.silu(gate) * up) @ w_down


def rope(x, cos, sin):
    x1, x2 = jnp.split(x, 2, axis=-1)
    rot = jnp.concatenate([-x2, x1], axis=-1)
    return x * cos[:, None, None, :] + rot * sin[:, None, None, :]


def gmlp_spatial_gating(u, v, ln_g, ln_b, w_s, b_s):
    B, S, _ = u.shape
    nb = S // GMLP_BLOCK
    u = jax.nn.gelu(u)
    v = layer_norm(jax.nn.gelu(v), ln_g, ln_b)
    v = v.reshape(B, nb, GMLP_BLOCK, GMLP_GROUPS, GMLP_GROUP_DIM)
    pos_chunk = jnp.arange(GMLP_BLOCK) // CHUNK
    mask = pos_chunk[None, :] <= pos_chunk[:, None]
    w = jnp.where(mask[None], w_s, jnp.zeros_like(w_s))
    mixed = jnp.einsum('gij,bnjgc->bnigc', w, v) + jnp.transpose(b_s)[None, None, :, :, None]
    out = u.reshape(B, nb, GMLP_BLOCK, GMLP_GROUPS, GMLP_GROUP_DIM) * mixed
    return out.reshape(B, S, GMLP_WIDTH)


def differential_attention(q, k, v, lam, lam_init, sub_g):
    B, S = q.shape[0], q.shape[1]
    nb = S // Q_BLOCK
    k_chunk = jnp.arange(S) // CHUNK
    q_chunk = k_chunk.reshape(nb, Q_BLOCK)
    qb = (q * (DIFF_QK_DIM ** -0.5)).reshape(B, nb, Q_BLOCK, DIFF_HEADS, 2, DIFF_QK_DIM)
    qb = jnp.transpose(qb, (1, 0, 2, 3, 4, 5))

    def one_block(args):
        q_blk, qc = args
        s = jnp.einsum('bqhcd,bkhcd->bhcqk', q_blk, k).astype(jnp.float32)
        mask = k_chunk[None, :] <= qc[:, None]
        s = jnp.where(mask, s, NEG_INF)
        p = jax.nn.softmax(s, axis=-1)
        a = p[:, :, 0] - lam * p[:, :, 1]
        return jnp.einsum('bhqk,bkhd->bqhd', a.astype(v.dtype), v)

    o = lax.map(one_block, (qb, q_chunk))
    o = jnp.transpose(o, (1, 0, 2, 3, 4)).reshape(B, S, DIFF_HEADS, DIFF_V_DIM)
    o = rms_norm(o, sub_g) * (1.0 - lam_init)
    return o.reshape(B, S, DIFF_WIDTH)


def hybrid_mixer(h, w_in, gmlp_ln_g, gmlp_ln_b, gmlp_w_s, gmlp_b_s,
                 lambda_q1, lambda_k1, lambda_q2, lambda_k2, diff_subln_g, w_out, layer_idx):
    B, S, _ = h.shape
    proj = h @ w_in
    c1 = GMLP_WIDTH
    c2 = 2 * GMLP_WIDTH
    c3 = c2 + DIFF_HEADS * 2 * DIFF_QK_DIM
    c4 = c3 + DIFF_HEADS * 2 * DIFF_QK_DIM
    u, vg, q, k, v = jnp.split(proj, [c1, c2, c3, c4], axis=-1)

    out_a = gmlp_spatial_gating(u, vg, gmlp_ln_g, gmlp_ln_b, gmlp_w_s, gmlp_b_s)

    q = q.reshape(B, S, DIFF_HEADS, 2, DIFF_QK_DIM)
    k = k.reshape(B, S, DIFF_HEADS, 2, DIFF_QK_DIM)
    v = v.reshape(B, S, DIFF_HEADS, DIFF_V_DIM)
    pos = jnp.arange(S, dtype=jnp.float32)
    inv_freq = ROPE_THETA ** (-jnp.arange(0, DIFF_QK_DIM, 2, dtype=jnp.float32) / DIFF_QK_DIM)
    ang = pos[:, None] * inv_freq[None, :]
    cos = jnp.concatenate([jnp.cos(ang), jnp.cos(ang)], axis=-1).astype(q.dtype)
    sin = jnp.concatenate([jnp.sin(ang), jnp.sin(ang)], axis=-1).astype(q.dtype)
    q = rope(q, cos, sin)
    k = rope(k, cos, sin)
    lam_init = 0.8 - 0.6 * math.exp(-0.3 * layer_idx)
    lam = (jnp.exp(jnp.sum(lambda_q1.astype(jnp.float32) * lambda_k1.astype(jnp.float32)))
           - jnp.exp(jnp.sum(lambda_q2.astype(jnp.float32) * lambda_k2.astype(jnp.float32)))
           + lam_init)
    out_b = differential_attention(q, k, v, lam, lam_init, diff_subln_g)

    return jnp.concatenate([out_a, out_b], axis=-1) @ w_out


def setup_inputs(seed: int = 0) -> dict:
    key = jax.random.key(seed)
    ks = jax.random.split(key, 32)
    f32 = jnp.float32
    L, D = DEPTH, D_MODEL

    def nrm(k, shape, scale):
        return jax.random.normal(k, shape, f32) * scale

    return {
        "x": nrm(ks[0], (BATCH, SEQ, D), 1.0),
        "c": nrm(ks[1], (BATCH, D), 1.0),
        "w_ada": nrm(ks[2], (L, D, N_MOD * D), 0.1 * D ** -0.5),
        "b_ada": nrm(ks[3], (L, N_MOD * D), 0.02),
        "ffn1_w_gu": nrm(ks[4], (L, D, 2 * D_FF), D ** -0.5),
        "ffn1_w_down": nrm(ks[5], (L, D_FF, D), BETA * D_FF ** -0.5),
        "ln1_g": 1.0 + nrm(ks[6], (L, D), 0.02),
        "ln1_b": nrm(ks[7], (L, D), 0.02),
        "w_in": nrm(ks[8], (L, D, IN_COLS), D ** -0.5),
        "gmlp_ln_g": 1.0 + nrm(ks[9], (L, GMLP_WIDTH), 0.02),
        "gmlp_ln_b": nrm(ks[10], (L, GMLP_WIDTH), 0.02),
        "gmlp_w_s": nrm(ks[11], (L, GMLP_GROUPS, GMLP_BLOCK, GMLP_BLOCK), 0.5 * GMLP_BLOCK ** -0.5),
        "gmlp_b_s": 1.0 + nrm(ks[12], (L, GMLP_GROUPS, GMLP_BLOCK), 0.1),
        "lambda_q1": nrm(ks[13], (L, DIFF_QK_DIM), 0.1),
        "lambda_k1": nrm(ks[14], (L, DIFF_QK_DIM), 0.1),
        "lambda_q2": nrm(ks[15], (L, DIFF_QK_DIM), 0.1),
        "lambda_k2": nrm(ks[16], (L, DIFF_QK_DIM), 0.1),
        "diff_subln_g": 1.0 + nrm(ks[17], (L, DIFF_V_DIM), 0.02),
        "w_out": nrm(ks[18], (L, MIX_WIDTH, D), BETA * MIX_WIDTH ** -0.5),
        "ln2_g": 1.0 + nrm(ks[19], (L, D), 0.02),
        "ln2_b": nrm(ks[20], (L, D), 0.02),
        "ffn2_w_gu": nrm(ks[21], (L, D, 2 * D_FF), D ** -0.5),
        "ffn2_w_down": nrm(ks[22], (L, D_FF, D), BETA * D_FF ** -0.5),
        "ln3_g": 1.0 + nrm(ks[23], (L, D), 0.02),
        "ln3_b": nrm(ks[24], (L, D), 0.02),
    }


def reference(x, c, w_ada, b_ada, ffn1_w_gu, ffn1_w_down, ln1_g, ln1_b,
              w_in, gmlp_ln_g, gmlp_ln_b, gmlp_w_s, gmlp_b_s,
              lambda_q1, lambda_k1, lambda_q2, lambda_k2, diff_subln_g, w_out,
              ln2_g, ln2_b, ffn2_w_gu, ffn2_w_down, ln3_g, ln3_b):
    c_act = jax.nn.silu(c)
    for l in range(DEPTH):
        mod = c_act @ w_ada[l] + b_ada[l]
        s1, sc1, g1, s2, sc2, g2, s3, sc3, g3 = jnp.split(mod, N_MOD, axis=-1)

        h = modulate(x, s1, sc1)
        y = swiglu_ffn(h, ffn1_w_gu[l], ffn1_w_down[l])
        x = layer_norm(ALPHA * x + 0.5 * (1 + g1)[:, None, :] * y, ln1_g[l], ln1_b[l])

        h = modulate(x, s2, sc2)
        y = hybrid_mixer(h, w_in[l], gmlp_ln_g[l], gmlp_ln_b[l], gmlp_w_s[l], gmlp_b_s[l],
                         lambda_q1[l], lambda_k1[l], lambda_q2[l], lambda_k2[l],
                         diff_subln_g[l], w_out[l], l)
        x = layer_norm(ALPHA * x + (1 + g2)[:, None, :] * y, ln2_g[l], ln2_b[l])

        h = modulate(x, s3, sc3)
        y = swiglu_ffn(h, ffn2_w_gu[l], ffn2_w_down[l])
        x = layer_norm(ALPHA * x + 0.5 * (1 + g3)[:, None, :] * y, ln3_g[l], ln3_b[l])
    return x
```

```python
import functools
import math

import jax
import jax.numpy as jnp
from jax import lax
from jax.experimental import pallas as pl
from jax.experimental.pallas import tpu as pltpu

F32 = jnp.float32
BF16 = jnp.bfloat16

CHUNK = 64
GMLP_BLOCK = 128
GMLP_GROUPS = 8
DIFF_HEADS = 8
DIFF_V_DIM = 128
DIFF_QK_DIM = 64
N_MOD = 9
ROPE_THETA = 10000.0
LN_EPS = 1e-5
NEG_INF = -1e30

LANES = 128
VMEM_LIMIT_BYTES = 56 * 1024 * 1024

ADA_TN = 1024
FFN_TM = 512
FFN_TF = 512
MIX_TM = 512
ATT_T = 512


def _params(*sem):
    return pltpu.CompilerParams(dimension_semantics=sem,
                                vmem_limit_bytes=VMEM_LIMIT_BYTES)


def _layer_norm_rows(z, g, b):
    mu = jnp.mean(z, axis=-1, keepdims=True)
    zc = z - mu
    var = jnp.mean(zc * zc, axis=-1, keepdims=True)
    return zc * lax.rsqrt(var + LN_EPS) * g + b


def _adaln_kernel(c_ref, w_ref, b_ref, o_ref):
    c = c_ref[...]
    ca = (c / (1.0 + jnp.exp(-c))).astype(BF16)
    o_ref[...] = jnp.dot(ca, w_ref[...].astype(BF16),
                         preferred_element_type=F32) + b_ref[...]


def _adaln(c_pad, w, b):
    d, n = w.shape
    return pl.pallas_call(
        _adaln_kernel,
        grid=(n // ADA_TN,),
        in_specs=[pl.BlockSpec((8, d), lambda j: (0, 0)),
                  pl.BlockSpec((d, ADA_TN), lambda j: (0, j)),
                  pl.BlockSpec((1, ADA_TN), lambda j: (0, j))],
        out_specs=pl.BlockSpec((8, ADA_TN), lambda j: (0, j)),
        out_shape=jax.ShapeDtypeStruct((8, n), F32),
        compiler_params=_params("arbitrary"),
        name="adaln",
    )(c_pad, w, b.reshape(1, n))


def _ffn_kernel(alpha, x_ref, sh_ref, sc_ref, gt_ref, wg_ref, wu_ref, wd_ref,
                lg_ref, lb_ref, o_ref, h_ref):
    j = pl.program_id(2)

    @pl.when(j == 0)
    def _():
        h_ref[...] = (x_ref[...] * (1.0 + sc_ref[...]) + sh_ref[...]).astype(BF16)
        o_ref[...] = jnp.zeros_like(o_ref)

    h = h_ref[...]
    g = jnp.dot(h, wg_ref[...], preferred_element_type=F32)
    u = jnp.dot(h, wu_ref[...], preferred_element_type=F32)
    a = (g / (1.0 + jnp.exp(-g)) * u).astype(BF16)
    o_ref[...] += jnp.dot(a, wd_ref[...], preferred_element_type=F32)

    @pl.when(j == pl.num_programs(2) - 1)
    def _():
        z = alpha * x_ref[...] + (0.5 * (1.0 + gt_ref[...])) * o_ref[...]
        o_ref[...] = _layer_norm_rows(z, lg_ref[...], lb_ref[...])


def _ffn_ln(x, shift, scale, gate, w_gu, w_down, ln_g, ln_b, alpha):
    bsz, s, d = x.shape
    f = w_down.shape[0]
    nf = f // FFN_TF
    vec = pl.BlockSpec((None, 1, d), lambda b, i, j: (b, 0, 0))
    row = pl.BlockSpec((1, d), lambda b, i, j: (0, 0))
    tile = pl.BlockSpec((None, FFN_TM, d), lambda b, i, j: (b, i, 0))
    return pl.pallas_call(
        functools.partial(_ffn_kernel, alpha),
        grid=(bsz, s // FFN_TM, nf),
        in_specs=[tile, vec, vec, vec,
                  pl.BlockSpec((d, FFN_TF), lambda b, i, j: (0, j)),
                  pl.BlockSpec((d, FFN_TF), lambda b, i, j: (0, nf + j)),
                  pl.BlockSpec((FFN_TF, d), lambda b, i, j: (j, 0)),
                  row, row],
        out_specs=tile,
        out_shape=jax.ShapeDtypeStruct((bsz, s, d), F32),
        scratch_shapes=[pltpu.VMEM((FFN_TM, d), BF16)],
        compiler_params=_params("arbitrary", "arbitrary", "arbitrary"),
        name="ffn_ln",
    )(x, shift, scale, gate, w_gu, w_gu, w_down, ln_g.reshape(1, d), ln_b.reshape(1, d))


def _gmlp_kernel(x_ref, sh_ref, sc_ref, wu_ref, wv_ref, lg_ref, lb_ref, ws_ref,
                 bst_ref, o_ref):
    h = (x_ref[...] * (1.0 + sc_ref[...]) + sh_ref[...]).astype(BF16)
    u = jax.nn.gelu(jnp.dot(h, wu_ref[...], preferred_element_type=F32))
    v = jax.nn.gelu(jnp.dot(h, wv_ref[...], preferred_element_type=F32))
    v = _layer_norm_rows(v, lg_ref[...], lb_ref[...]).astype(BF16)

    pi = lax.broadcasted_iota(jnp.int32, (GMLP_BLOCK, GMLP_BLOCK), 0) // CHUNK
    pj = lax.broadcasted_iota(jnp.int32, (GMLP_BLOCK, GMLP_BLOCK), 1) // CHUNK
    keep = pj <= pi
    bst = bst_ref[...]
    for g in range(GMLP_GROUPS):
        w = jnp.where(keep, ws_ref[g], 0.0).astype(BF16)
        bias = bst[:, g:g + 1]
        cols = slice(g * LANES, (g + 1) * LANES)
        for blk in range(x_ref.shape[0] // GMLP_BLOCK):
            rows = slice(blk * GMLP_BLOCK, (blk + 1) * GMLP_BLOCK)
            mixed = jnp.dot(w, v[rows, cols], preferred_element_type=F32) + bias
            o_ref[rows, cols] = (u[rows, cols] * mixed).astype(BF16)


def _gmlp(x, shift, scale, w_u, w_v, ln_g, ln_b, w_s, b_s_t):
    bsz, s, d = x.shape
    width = w_u.shape[1]
    vec = pl.BlockSpec((None, 1, d), lambda b, i: (b, 0, 0))
    full = lambda shape: pl.BlockSpec(shape, lambda b, i: (0,) * len(shape))
    return pl.pallas_call(
        _gmlp_kernel,
        grid=(bsz, s // MIX_TM),
        in_specs=[pl.BlockSpec((None, MIX_TM, d), lambda b, i: (b, i, 0)), vec, vec,
                  full((d, width)), full((d, width)),
                  full((1, width)), full((1, width)),
                  full(w_s.shape), full(b_s_t.shape)],
        out_specs=pl.BlockSpec((None, MIX_TM, width), lambda b, i: (b, i, 0)),
        out_shape=jax.ShapeDtypeStruct((bsz, s, width), BF16),
        compiler_params=_params("arbitrary", "arbitrary"),
        name="gmlp",
    )(x, shift, scale, w_u, w_v, ln_g.reshape(1, width), ln_b.reshape(1, width),
      w_s, b_s_t)


def _rope(x, cos, sin_signed, first_half):
    rot = jnp.where(first_half, pltpu.roll(x, 3 * LANES // 4, 1),
                    pltpu.roll(x, LANES // 4, 1))
    return x * cos + rot * sin_signed


def _qkv_kernel(x_ref, sh_ref, sc_ref, wq_ref, wk_ref, wv_ref, cos_ref, sin_ref,
                qq_ref, k_ref, vt_ref):
    t = x_ref.shape[0]
    h = (x_ref[...] * (1.0 + sc_ref[...]) + sh_ref[...]).astype(BF16)
    q = jnp.dot(h, wq_ref[...], preferred_element_type=F32)
    k = jnp.dot(h, wk_ref[...], preferred_element_type=F32)
    v = jnp.dot(h, wv_ref[...], preferred_element_type=F32)
    cos = cos_ref[...]
    sin = sin_ref[...]
    lane = lax.broadcasted_iota(jnp.int32, (t, LANES), 1)
    first_half = (lane % DIFF_QK_DIM) < (DIFF_QK_DIM // 2)
    comp = lax.broadcasted_iota(jnp.int32, (LANES, t), 0) < DIFF_QK_DIM
    q_scale = DIFF_QK_DIM ** -0.5
    for hd in range(DIFF_HEADS):
        cols = slice(hd * LANES, (hd + 1) * LANES)
        k_ref[:, cols] = _rope(k[:, cols], cos, sin, first_half).astype(BF16)
        qt = (_rope(q[:, cols], cos, sin, first_half) * q_scale).T
        qq_ref[hd, 0, :, :t] = jnp.where(comp, qt, 0.0).astype(BF16)
        qq_ref[hd, 0, :, t:] = jnp.where(comp, 0.0, qt).astype(BF16)
        vt_ref[hd, 0] = v[:, cols].T.astype(BF16)


def _qkv_rope(x, shift, scale, w_q, w_k, w_v, cos, sin_signed):
    bsz, s, d = x.shape
    t = ATT_T
    nt = s // t
    width = w_q.shape[1]
    vec = pl.BlockSpec((None, 1, d), lambda b, i: (b, 0, 0))
    wspec = pl.BlockSpec((d, width), lambda b, i: (0, 0))
    tab = pl.BlockSpec((t, LANES), lambda b, i: (i, 0))
    return pl.pallas_call(
        _qkv_kernel,
        grid=(bsz, nt),
        in_specs=[pl.BlockSpec((None, t, d), lambda b, i: (b, i, 0)), vec, vec,
                  wspec, wspec, wspec, tab, tab],
        out_specs=[
            pl.BlockSpec((None, DIFF_HEADS, 1, LANES, 2 * t), lambda b, i: (b, 0, i, 0, 0)),
            pl.BlockSpec((None, t, width), lambda b, i: (b, i, 0)),
            pl.BlockSpec((None, DIFF_HEADS, 1, LANES, t), lambda b, i: (b, 0, i, 0, 0)),
        ],
        out_shape=[
            jax.ShapeDtypeStruct((bsz, DIFF_HEADS, nt, LANES, 2 * t), BF16),
            jax.ShapeDtypeStruct((bsz, s, width), BF16),
            jax.ShapeDtypeStruct((bsz, DIFF_HEADS, nt, LANES, t), BF16),
        ],
        compiler_params=_params("arbitrary", "arbitrary"),
        name="qkv_rope",
    )(x, shift, scale, w_q, w_k, w_v, cos, sin_signed)


def _attn_kernel(lam_init, qq_ref, k_ref, vt_ref, lq1_ref, lk1_ref, lq2_ref, lk2_ref,
                 sg_ref, o_ref, m_ref, l_ref, acc_ref):
    t = ATT_T
    i = pl.program_id(2)
    qq = qq_ref[...]

    m_ref[...] = jnp.full_like(m_ref, NEG_INF)
    l_ref[...] = jnp.zeros_like(l_ref)
    acc_ref[...] = jnp.zeros_like(acc_ref)

    def step(j, masked):
        kblk = k_ref[pl.ds(pl.multiple_of(j * t, t), t), :]
        s = jnp.dot(kblk, qq, preferred_element_type=F32)
        if masked:
            kc = lax.broadcasted_iota(jnp.int32, (t, 2 * t), 0) // CHUNK
            qc = (lax.broadcasted_iota(jnp.int32, (t, 2 * t), 1) % t) // CHUNK
            s = jnp.where(kc <= qc, s, NEG_INF)
        m_prev = m_ref[...]
        m_new = jnp.maximum(m_prev, jnp.max(s, axis=0, keepdims=True))
        alpha = jnp.exp(m_prev - m_new)
        p = jnp.exp(s - m_new)
        l_ref[...] = alpha * l_ref[...] + jnp.sum(p, axis=0, keepdims=True)
        pv = jnp.dot(vt_ref[j], p.astype(BF16), preferred_element_type=F32)
        acc_ref[...] = alpha * acc_ref[...] + pv
        m_ref[...] = m_new

    def body(j, carry):
        step(j, masked=False)
        return carry

    lax.fori_loop(0, i, body, 0)
    step(i, masked=True)

    lam = (jnp.exp(jnp.sum(lq1_ref[...] * lk1_ref[...], keepdims=True))
           - jnp.exp(jnp.sum(lq2_ref[...] * lk2_ref[...], keepdims=True))
           + lam_init)
    o = acc_ref[...] / l_ref[...]
    o = o[:, :t] - lam * o[:, t:]
    ms = jnp.mean(o * o, axis=0, keepdims=True)
    o = o * lax.rsqrt(ms + LN_EPS) * sg_ref[...] * (1.0 - lam_init)
    o_ref[...] = o.T.astype(BF16)


def _diff_attn(qq, k, vt, lq1, lk1, lq2, lk2, sub_g, lam_init):
    bsz, heads, nt, _, _ = qq.shape
    t = ATT_T
    s = nt * t
    lam_spec = pl.BlockSpec((1, DIFF_QK_DIM), lambda b, h, i: (0, 0))
    return pl.pallas_call(
        functools.partial(_attn_kernel, lam_init),
        grid=(bsz, heads, nt),
        in_specs=[
            pl.BlockSpec((None, None, None, LANES, 2 * t), lambda b, h, i: (b, h, i, 0, 0)),
            pl.BlockSpec((None, s, LANES), lambda b, h, i: (b, 0, h)),
            pl.BlockSpec((None, None, nt, LANES, t), lambda b, h, i: (b, h, 0, 0, 0)),
            lam_spec, lam_spec, lam_spec, lam_spec,
            pl.BlockSpec((DIFF_V_DIM, 1), lambda b, h, i: (0, 0)),
        ],
        out_specs=pl.BlockSpec((None, t, LANES), lambda b, h, i: (b, i, h)),
        out_shape=jax.ShapeDtypeStruct((bsz, s, heads * DIFF_V_DIM), BF16),
        scratch_shapes=[pltpu.VMEM((1, 2 * t), F32), pltpu.VMEM((1, 2 * t), F32),
                        pltpu.VMEM((DIFF_V_DIM, 2 * t), F32)],
        compiler_params=_params("arbitrary", "arbitrary", "arbitrary"),
        name="diff_attn",
    )(qq, k, vt, lq1.reshape(1, -1), lk1.reshape(1, -1), lq2.reshape(1, -1),
      lk2.reshape(1, -1), sub_g.reshape(-1, 1))


def _out_kernel(alpha, x_ref, gt_ref, a_ref, b_ref, wa_ref, wb_ref, lg_ref, lb_ref,
                o_ref):
    y = jnp.dot(a_ref[...], wa_ref[...], preferred_element_type=F32)
    y = y + jnp.dot(b_ref[...], wb_ref[...], preferred_element_type=F32)
    z = alpha * x_ref[...] + (1.0 + gt_ref[...]) * y
    o_ref[...] = _layer_norm_rows(z, lg_ref[...], lb_ref[...])


def _out_ln(x, gate, out_a, out_b, w_out, ln_g, ln_b, alpha):
    bsz, s, d = x.shape
    half = out_a.shape[-1]
    tile = pl.BlockSpec((None, MIX_TM, d), lambda b, i: (b, i, 0))
    act = pl.BlockSpec((None, MIX_TM, half), lambda b, i: (b, i, 0))
    row = pl.BlockSpec((1, d), lambda b, i: (0, 0))
    return pl.pallas_call(
        functools.partial(_out_kernel, alpha),
        grid=(bsz, s // MIX_TM),
        in_specs=[tile, pl.BlockSpec((None, 1, d), lambda b, i: (b, 0, 0)), act, act,
                  pl.BlockSpec((half, d), lambda b, i: (0, 0)),
                  pl.BlockSpec((half, d), lambda b, i: (1, 0)),
                  row, row],
        out_specs=tile,
        out_shape=jax.ShapeDtypeStruct((bsz, s, d), F32),
        compiler_params=_params("arbitrary", "arbitrary"),
        name="out_ln",
    )(x, gate, out_a, out_b, w_out, w_out, ln_g.reshape(1, d), ln_b.reshape(1, d))


def _rope_tables(s):
    half = DIFF_QK_DIM // 2
    pos = jnp.arange(s, dtype=F32)
    inv_freq = ROPE_THETA ** (-jnp.arange(0, DIFF_QK_DIM, 2, dtype=F32) / DIFF_QK_DIM)
    ang = pos[:, None] * inv_freq[None, :]
    cos, sin = jnp.cos(ang), jnp.sin(ang)
    cos = jnp.concatenate([cos, cos, cos, cos], axis=-1)
    sin_signed = jnp.concatenate([-sin, sin, -sin, sin], axis=-1)
    assert cos.shape[-1] == 4 * half == LANES
    return cos, sin_signed


def kernel(x, c, w_ada, b_ada, ffn1_w_gu, ffn1_w_down, ln1_g, ln1_b, w_in, gmlp_ln_g, gmlp_ln_b, gmlp_w_s, gmlp_b_s, lambda_q1, lambda_k1, lambda_q2, lambda_k2, diff_subln_g, w_out, ln2_g, ln2_b, ffn2_w_gu, ffn2_w_down, ln3_g, ln3_b):
    bsz, s, d = x.shape
    depth = w_ada.shape[0]
    alpha = (2 * depth) ** 0.25
    gw = gmlp_ln_g.shape[-1]
    qk = DIFF_HEADS * 2 * DIFF_QK_DIM
    c1, c2, c3, c4 = gw, 2 * gw, 2 * gw + qk, 2 * gw + 2 * qk
    cos, sin_signed = _rope_tables(s)
    c_pad = jnp.zeros((8, d), F32).at[:bsz].set(c)

    for l in range(depth):
        mod = _adaln(c_pad, w_ada[l], b_ada[l])[:bsz]
        mod = mod.reshape(bsz, N_MOD, 1, d)
        s1, sc1, g1, s2, sc2, g2, s3, sc3, g3 = [mod[:, n] for n in range(N_MOD)]

        x = _ffn_ln(x, s1, sc1, g1, ffn1_w_gu[l].astype(BF16),
                    ffn1_w_down[l].astype(BF16), ln1_g[l], ln1_b[l], alpha)

        w_in_b = w_in[l].astype(BF16)
        out_a = _gmlp(x, s2, sc2, w_in_b[:, :c1], w_in_b[:, c1:c2], gmlp_ln_g[l],
                      gmlp_ln_b[l], gmlp_w_s[l], jnp.transpose(gmlp_b_s[l]))
        qq, k, vt = _qkv_rope(x, s2, sc2, w_in_b[:, c2:c3], w_in_b[:, c3:c4],
                              w_in_b[:, c4:], cos, sin_signed)
        lam_init = 0.8 - 0.6 * math.exp(-0.3 * l)
        out_b = _diff_attn(qq, k, vt, lambda_q1[l], lambda_k1[l], lambda_q2[l],
                           lambda_k2[l], diff_subln_g[l], lam_init)
        x = _out_ln(x, g2, out_a, out_b, w_out[l].astype(BF16), ln2_g[l], ln2_b[l], alpha)

        x = _ffn_ln(x, s3, sc3, g3, ffn2_w_gu[l].astype(BF16),
                    ffn2_w_down[l].astype(BF16), ln3_g[l], ln3_b[l], alpha)
    return x
```

```python
import functools
import math

import jax
import jax.numpy as jnp
from jax import lax
from jax.experimental import pallas as pl
from jax.experimental.pallas import tpu as pltpu

F32 = jnp.float32
BF16 = jnp.bfloat16

CHUNK = 64
GMLP_BLOCK = 128
GMLP_GROUPS = 8
DIFF_HEADS = 8
DIFF_V_DIM = 128
DIFF_QK_DIM = 64
N_MOD = 9
ROPE_THETA = 10000.0
LN_EPS = 1e-5
NEG_INF = -1e30
LOG2_E = 1.4426950408889634
V_AUG_ROWS = DIFF_V_DIM + 16

LANES = 128
VMEM_LIMIT_BYTES = 56 * 1024 * 1024

ADA_TN = 1024
FFN_TM = 512
FFN_TF = 512
MIX_TM = 512
ATT_T = 512


def _params(*sem):
    return pltpu.CompilerParams(dimension_semantics=sem,
                                vmem_limit_bytes=VMEM_LIMIT_BYTES)


def _layer_norm_rows(z, g, b):
    mu = jnp.mean(z, axis=-1, keepdims=True)
    zc = z - mu
    var = jnp.mean(zc * zc, axis=-1, keepdims=True)
    return zc * lax.rsqrt(var + LN_EPS) * g + b


def _adaln_kernel(c_ref, w_ref, b_ref, o_ref):
    c = c_ref[...]
    ca = (c / (1.0 + jnp.exp(-c))).astype(BF16)
    o_ref[...] = jnp.dot(ca, w_ref[...].astype(BF16),
                         preferred_element_type=F32) + b_ref[...]


def _adaln(c_pad, w, b):
    d, n = w.shape
    return pl.pallas_call(
        _adaln_kernel,
        grid=(n // ADA_TN,),
        in_specs=[pl.BlockSpec((8, d), lambda j: (0, 0)),
                  pl.BlockSpec((d, ADA_TN), lambda j: (0, j)),
                  pl.BlockSpec((1, ADA_TN), lambda j: (0, j))],
        out_specs=pl.BlockSpec((8, ADA_TN), lambda j: (0, j)),
        out_shape=jax.ShapeDtypeStruct((8, n), F32),
        compiler_params=_params("arbitrary"),
        name="adaln",
    )(c_pad, w, b.reshape(1, n))


def _ffn_kernel(alpha, x_ref, sh_ref, sc_ref, gt_ref, wg_ref, wu_ref, wd_ref,
                lg_ref, lb_ref, o_ref, h_ref):
    j = pl.program_id(2)

    @pl.when(j == 0)
    def _():
        h_ref[...] = (x_ref[...] * (1.0 + sc_ref[...]) + sh_ref[...]).astype(BF16)
        o_ref[...] = jnp.zeros_like(o_ref)

    h = h_ref[...]
    g = jnp.dot(h, wg_ref[...], preferred_element_type=F32)
    u = jnp.dot(h, wu_ref[...], preferred_element_type=F32)
    a = (g / (1.0 + jnp.exp(-g)) * u).astype(BF16)
    o_ref[...] += jnp.dot(a, wd_ref[...], preferred_element_type=F32)

    @pl.when(j == pl.num_programs(2) - 1)
    def _():
        z = alpha * x_ref[...] + (0.5 * (1.0 + gt_ref[...])) * o_ref[...]
        o_ref[...] = _layer_norm_rows(z, lg_ref[...], lb_ref[...])


def _ffn_ln(x, shift, scale, gate, w_gu, w_down, ln_g, ln_b, alpha):
    bsz, s, d = x.shape
    f = w_down.shape[0]
    nf = f // FFN_TF
    vec = pl.BlockSpec((None, 1, d), lambda b, i, j: (b, 0, 0))
    row = pl.BlockSpec((1, d), lambda b, i, j: (0, 0))
    tile = pl.BlockSpec((None, FFN_TM, d), lambda b, i, j: (b, i, 0))
    return pl.pallas_call(
        functools.partial(_ffn_kernel, alpha),
        grid=(bsz, s // FFN_TM, nf),
        in_specs=[tile, vec, vec, vec,
                  pl.BlockSpec((d, FFN_TF), lambda b, i, j: (0, j)),
                  pl.BlockSpec((d, FFN_TF), lambda b, i, j: (0, nf + j)),
                  pl.BlockSpec((FFN_TF, d), lambda b, i, j: (j, 0)),
                  row, row],
        out_specs=tile,
        out_shape=jax.ShapeDtypeStruct((bsz, s, d), F32),
        scratch_shapes=[pltpu.VMEM((FFN_TM, d), BF16)],
        compiler_params=_params("arbitrary", "arbitrary", "arbitrary"),
        name="ffn_ln",
    )(x, shift, scale, gate, w_gu, w_gu, w_down, ln_g.reshape(1, d), ln_b.reshape(1, d))


def _gmlp_kernel(x_ref, sh_ref, sc_ref, wu_ref, wv_ref, lg_ref, lb_ref, ws_ref,
                 bst_ref, o_ref):
    h = (x_ref[...] * (1.0 + sc_ref[...]) + sh_ref[...]).astype(BF16)
    u = jax.nn.gelu(jnp.dot(h, wu_ref[...], preferred_element_type=F32))
    v = jax.nn.gelu(jnp.dot(h, wv_ref[...], preferred_element_type=F32))
    v = _layer_norm_rows(v, lg_ref[...], lb_ref[...]).astype(BF16)

    pi = lax.broadcasted_iota(jnp.int32, (GMLP_BLOCK, GMLP_BLOCK), 0) // CHUNK
    pj = lax.broadcasted_iota(jnp.int32, (GMLP_BLOCK, GMLP_BLOCK), 1) // CHUNK
    keep = pj <= pi
    bst = bst_ref[...]
    for g in range(GMLP_GROUPS):
        w = jnp.where(keep, ws_ref[g], 0.0).astype(BF16)
        bias = bst[:, g:g + 1]
        cols = slice(g * LANES, (g + 1) * LANES)
        for blk in range(x_ref.shape[0] // GMLP_BLOCK):
            rows = slice(blk * GMLP_BLOCK, (blk + 1) * GMLP_BLOCK)
            mixed = jnp.dot(w, v[rows, cols], preferred_element_type=F32) + bias
            o_ref[rows, cols] = (u[rows, cols] * mixed).astype(BF16)


def _gmlp(x, shift, scale, w_u, w_v, ln_g, ln_b, w_s, b_s_t):
    bsz, s, d = x.shape
    width = w_u.shape[1]
    vec = pl.BlockSpec((None, 1, d), lambda b, i: (b, 0, 0))
    full = lambda shape: pl.BlockSpec(shape, lambda b, i: (0,) * len(shape))
    return pl.pallas_call(
        _gmlp_kernel,
        grid=(bsz, s // MIX_TM),
        in_specs=[pl.BlockSpec((None, MIX_TM, d), lambda b, i: (b, i, 0)), vec, vec,
                  full((d, width)), full((d, width)),
                  full((1, width)), full((1, width)),
                  full(w_s.shape), full(b_s_t.shape)],
        out_specs=pl.BlockSpec((None, MIX_TM, width), lambda b, i: (b, i, 0)),
        out_shape=jax.ShapeDtypeStruct((bsz, s, width), BF16),
        compiler_params=_params("arbitrary", "arbitrary"),
        name="gmlp",
    )(x, shift, scale, w_u, w_v, ln_g.reshape(1, width), ln_b.reshape(1, width),
      w_s, b_s_t)


def _rope(x, cos, sin_signed, first_half):
    rot = jnp.where(first_half, pltpu.roll(x, 3 * LANES // 4, 1),
                    pltpu.roll(x, LANES // 4, 1))
    return x * cos + rot * sin_signed


def _qkv_kernel(x_ref, sh_ref, sc_ref, wq_ref, wk_ref, wv_ref, cos_ref, sin_ref,
                qq_ref, k_ref, vt_ref):
    t = x_ref.shape[0]
    h = (x_ref[...] * (1.0 + sc_ref[...]) + sh_ref[...]).astype(BF16)
    q = jnp.dot(h, wq_ref[...], preferred_element_type=F32)
    k = jnp.dot(h, wk_ref[...], preferred_element_type=F32)
    v = jnp.dot(h, wv_ref[...], preferred_element_type=F32)
    cos = cos_ref[...]
    sin = sin_ref[...]
    lane = lax.broadcasted_iota(jnp.int32, (t, LANES), 1)
    first_half = (lane % DIFF_QK_DIM) < (DIFF_QK_DIM // 2)
    comp = lax.broadcasted_iota(jnp.int32, (LANES, t), 0) < DIFF_QK_DIM
    q_scale = DIFF_QK_DIM ** -0.5 * LOG2_E
    ones = jnp.ones((V_AUG_ROWS - DIFF_V_DIM, t), BF16)
    for hd in range(DIFF_HEADS):
        cols = slice(hd * LANES, (hd + 1) * LANES)
        k_ref[:, cols] = _rope(k[:, cols], cos, sin, first_half).astype(BF16)
        qt = (_rope(q[:, cols], cos, sin, first_half) * q_scale).T
        qq_ref[hd, 0, :, :t] = jnp.where(comp, qt, 0.0).astype(BF16)
        qq_ref[hd, 0, :, t:] = jnp.where(comp, 0.0, qt).astype(BF16)
        vt_ref[hd, 0, :DIFF_V_DIM, :] = v[:, cols].T.astype(BF16)
        vt_ref[hd, 0, DIFF_V_DIM:, :] = ones


def _qkv_rope(x, shift, scale, w_q, w_k, w_v, cos, sin_signed):
    bsz, s, d = x.shape
    t = ATT_T
    nt = s // t
    width = w_q.shape[1]
    vec = pl.BlockSpec((None, 1, d), lambda b, i: (b, 0, 0))
    wspec = pl.BlockSpec((d, width), lambda b, i: (0, 0))
    tab = pl.BlockSpec((t, LANES), lambda b, i: (i, 0))
    return pl.pallas_call(
        _qkv_kernel,
        grid=(bsz, nt),
        in_specs=[pl.BlockSpec((None, t, d), lambda b, i: (b, i, 0)), vec, vec,
                  wspec, wspec, wspec, tab, tab],
        out_specs=[
            pl.BlockSpec((None, DIFF_HEADS, 1, LANES, 2 * t), lambda b, i: (b, 0, i, 0, 0)),
            pl.BlockSpec((None, t, width), lambda b, i: (b, i, 0)),
            pl.BlockSpec((None, DIFF_HEADS, 1, V_AUG_ROWS, t), lambda b, i: (b, 0, i, 0, 0)),
        ],
        out_shape=[
            jax.ShapeDtypeStruct((bsz, DIFF_HEADS, nt, LANES, 2 * t), BF16),
            jax.ShapeDtypeStruct((bsz, s, width), BF16),
            jax.ShapeDtypeStruct((bsz, DIFF_HEADS, nt, V_AUG_ROWS, t), BF16),
        ],
        compiler_params=_params("arbitrary", "arbitrary"),
        name="qkv_rope",
    )(x, shift, scale, w_q, w_k, w_v, cos, sin_signed)


def _attn_kernel(lam_init, qq_ref, k_ref, vt_ref, lq1_ref, lk1_ref, lq2_ref, lk2_ref,
                 sg_ref, o_ref, m_ref, acc_ref, s0_ref, s1_ref):
    t = ATT_T
    i = pl.program_id(2)
    qq = qq_ref[...]

    m_ref[...] = jnp.full_like(m_ref, NEG_INF)
    acc_ref[...] = jnp.zeros_like(acc_ref)

    def scores(j):
        kblk = k_ref[pl.ds(pl.multiple_of(j * t, t), t), :]
        return jnp.dot(kblk, qq, preferred_element_type=F32)

    def consume(s, j, masked):
        if masked:
            kc = lax.broadcasted_iota(jnp.int32, (t, 2 * t), 0) // CHUNK
            qc = (lax.broadcasted_iota(jnp.int32, (t, 2 * t), 1) % t) // CHUNK
            s = jnp.where(kc <= qc, s, NEG_INF)
        m_prev = m_ref[...]
        m_new = jnp.maximum(m_prev, jnp.max(s, axis=0, keepdims=True))
        alpha = jnp.exp2(m_prev - m_new)
        p = jnp.exp2(s - m_new).astype(BF16)
        pv = jnp.dot(vt_ref[j], p, preferred_element_type=F32)
        acc_ref[...] = alpha * acc_ref[...] + pv
        m_ref[...] = m_new

    s0_ref[...] = scores(0)

    def pair(p, carry):
        j = 2 * p
        s1_ref[...] = scores(j + 1)
        consume(s0_ref[...], j, masked=False)
        s0_ref[...] = scores(j + 2)
        consume(s1_ref[...], j + 1, masked=False)
        return carry

    lax.fori_loop(0, i // 2, pair, 0)

    @pl.when(i % 2 == 0)
    def _():
        consume(s0_ref[...], i, masked=True)

    @pl.when(i % 2 == 1)
    def _():
        s1_ref[...] = scores(i)
        consume(s0_ref[...], i - 1, masked=False)
        consume(s1_ref[...], i, masked=True)

    lam = (jnp.exp(jnp.sum(lq1_ref[...] * lk1_ref[...], keepdims=True))
           - jnp.exp(jnp.sum(lq2_ref[...] * lk2_ref[...], keepdims=True))
           + lam_init)
    o = acc_ref[:DIFF_V_DIM, :] / acc_ref[DIFF_V_DIM:DIFF_V_DIM + 1, :]
    o = o[:, :t] - lam * o[:, t:]
    ms = jnp.mean(o * o, axis=0, keepdims=True)
    o = o * lax.rsqrt(ms + LN_EPS) * sg_ref[...] * (1.0 - lam_init)
    o_ref[...] = o.T.astype(BF16)


def _diff_attn(qq, k, vt, lq1, lk1, lq2, lk2, sub_g, lam_init):
    bsz, heads, nt, _, _ = qq.shape
    t = ATT_T
    s = nt * t
    lam_spec = pl.BlockSpec((1, DIFF_QK_DIM), lambda b, h, i: (0, 0))
    return pl.pallas_call(
        functools.partial(_attn_kernel, lam_init),
        grid=(bsz, heads, nt),
        in_specs=[
            pl.BlockSpec((None, None, None, LANES, 2 * t), lambda b, h, i: (b, h, i, 0, 0)),
            pl.BlockSpec((None, s, LANES), lambda b, h, i: (b, 0, h)),
            pl.BlockSpec((None, None, nt, V_AUG_ROWS, t), lambda b, h, i: (b, h, 0, 0, 0)),
            lam_spec, lam_spec, lam_spec, lam_spec,
            pl.BlockSpec((DIFF_V_DIM, 1), lambda b, h, i: (0, 0)),
        ],
        out_specs=pl.BlockSpec((None, t, LANES), lambda b, h, i: (b, i, h)),
        out_shape=jax.ShapeDtypeStruct((bsz, s, heads * DIFF_V_DIM), BF16),
        scratch_shapes=[pltpu.VMEM((1, 2 * t), F32),
                        pltpu.VMEM((V_AUG_ROWS, 2 * t), F32),
                        pltpu.VMEM((t, 2 * t), F32), pltpu.VMEM((t, 2 * t), F32)],
        compiler_params=_params("arbitrary", "arbitrary", "arbitrary"),
        name="diff_attn",
    )(qq, k, vt, lq1.reshape(1, -1), lk1.reshape(1, -1), lq2.reshape(1, -1),
      lk2.reshape(1, -1), sub_g.reshape(-1, 1))


def _out_kernel(alpha, x_ref, gt_ref, a_ref, b_ref, wa_ref, wb_ref, lg_ref, lb_ref,
                o_ref):
    y = jnp.dot(a_ref[...], wa_ref[...], preferred_element_type=F32)
    y = y + jnp.dot(b_ref[...], wb_ref[...], preferred_element_type=F32)
    z = alpha * x_ref[...] + (1.0 + gt_ref[...]) * y
    o_ref[...] = _layer_norm_rows(z, lg_ref[...], lb_ref[...])


def _out_ln(x, gate, out_a, out_b, w_out, ln_g, ln_b, alpha):
    bsz, s, d = x.shape
    half = out_a.shape[-1]
    tile = pl.BlockSpec((None, MIX_TM, d), lambda b, i: (b, i, 0))
    act = pl.BlockSpec((None, MIX_TM, half), lambda b, i: (b, i, 0))
    row = pl.BlockSpec((1, d), lambda b, i: (0, 0))
    return pl.pallas_call(
        functools.partial(_out_kernel, alpha),
        grid=(bsz, s // MIX_TM),
        in_specs=[tile, pl.BlockSpec((None, 1, d), lambda b, i: (b, 0, 0)), act, act,
                  pl.BlockSpec((half, d), lambda b, i: (0, 0)),
                  pl.BlockSpec((half, d), lambda b, i: (1, 0)),
                  row, row],
        out_specs=tile,
        out_shape=jax.ShapeDtypeStruct((bsz, s, d), F32),
        compiler_params=_params("arbitrary", "arbitrary"),
        name="out_ln",
    )(x, gate, out_a, out_b, w_out, w_out, ln_g.reshape(1, d), ln_b.reshape(1, d))


def _rope_tables(s):
    half = DIFF_QK_DIM // 2
    pos = jnp.arange(s, dtype=F32)
    inv_freq = ROPE_THETA ** (-jnp.arange(0, DIFF_QK_DIM, 2, dtype=F32) / DIFF_QK_DIM)
    ang = pos[:, None] * inv_freq[None, :]
    cos, sin = jnp.cos(ang), jnp.sin(ang)
    cos = jnp.concatenate([cos, cos, cos, cos], axis=-1)
    sin_signed = jnp.concatenate([-sin, sin, -sin, sin], axis=-1)
    assert cos.shape[-1] == 4 * half == LANES
    return cos, sin_signed


def kernel(x, c, w_ada, b_ada, ffn1_w_gu, ffn1_w_down, ln1_g, ln1_b, w_in, gmlp_ln_g, gmlp_ln_b, gmlp_w_s, gmlp_b_s, lambda_q1, lambda_k1, lambda_q2, lambda_k2, diff_subln_g, w_out, ln2_g, ln2_b, ffn2_w_gu, ffn2_w_down, ln3_g, ln3_b):
    bsz, s, d = x.shape
    depth = w_ada.shape[0]
    alpha = (2 * depth) ** 0.25
    gw = gmlp_ln_g.shape[-1]
    qk = DIFF_HEADS * 2 * DIFF_QK_DIM
    c1, c2, c3, c4 = gw, 2 * gw, 2 * gw + qk, 2 * gw + 2 * qk
    cos, sin_signed = _rope_tables(s)
    c_pad = jnp.zeros((8, d), F32).at[:bsz].set(c)

    for l in range(depth):
        mod = _adaln(c_pad, w_ada[l], b_ada[l])[:bsz]
        mod = mod.reshape(bsz, N_MOD, 1, d)
        s1, sc1, g1, s2, sc2, g2, s3, sc3, g3 = [mod[:, n] for n in range(N_MOD)]

        x = _ffn_ln(x, s1, sc1, g1, ffn1_w_gu[l].astype(BF16),
                    ffn1_w_down[l].astype(BF16), ln1_g[l], ln1_b[l], alpha)

        w_in_b = w_in[l].astype(BF16)
        out_a = _gmlp(x, s2, sc2, w_in_b[:, :c1], w_in_b[:, c1:c2], gmlp_ln_g[l],
                      gmlp_ln_b[l], gmlp_w_s[l], jnp.transpose(gmlp_b_s[l]))
        qq, k, vt = _qkv_rope(x, s2, sc2, w_in_b[:, c2:c3], w_in_b[:, c3:c4],
                              w_in_b[:, c4:], cos, sin_signed)
        lam_init = 0.8 - 0.6 * math.exp(-0.3 * l)
        out_b = _diff_attn(qq, k, vt, lambda_q1[l], lambda_k1[l], lambda_q2[l],
                           lambda_k2[l], diff_subln_g[l], lam_init)
        x = _out_ln(x, g2, out_a, out_b, w_out[l].astype(BF16), ln2_g[l], ln2_b[l], alpha)

        x = _ffn_ln(x, s3, sc3, g3, ffn2_w_gu[l].astype(BF16),
                    ffn2_w_down[l].astype(BF16), ln3_g[l], ln3_b[l], alpha)
    return x
```

```python
import functools
import math

import jax
import jax.numpy as jnp
from jax import lax
from jax.experimental import pallas as pl
from jax.experimental.pallas import tpu as pltpu

F32 = jnp.float32
BF16 = jnp.bfloat16

CHUNK = 64
GMLP_BLOCK = 128
GMLP_GROUPS = 8
DIFF_HEADS = 8
DIFF_V_DIM = 128
DIFF_QK_DIM = 64
N_MOD = 9
ROPE_THETA = 10000.0
LN_EPS = 1e-5
NEG_INF = -1e30
LOG2_E = 1.4426950408889634
V_AUG_ROWS = DIFF_V_DIM + 16

LANES = 128
VMEM_LIMIT_BYTES = 56 * 1024 * 1024

ADA_TN = 1024
FFN_TM = 512
FFN_TF = 512
MIX_TM = 512
ROW_CHUNK = 256
ATT_T = 512
ATT_UNROLL = 8


def _params(*sem):
    return pltpu.CompilerParams(dimension_semantics=sem,
                                vmem_limit_bytes=VMEM_LIMIT_BYTES)


def _layer_norm_rows(z, g, b):
    mu = jnp.mean(z, axis=-1, keepdims=True)
    zc = z - mu
    var = jnp.mean(zc * zc, axis=-1, keepdims=True)
    return zc * lax.rsqrt(var + LN_EPS) * g + b


def _adaln_kernel(c_ref, w_ref, b_ref, o_ref):
    c = c_ref[...]
    ca = (c / (1.0 + jnp.exp(-c))).astype(BF16)
    o_ref[...] = jnp.dot(ca, w_ref[...].astype(BF16),
                         preferred_element_type=F32) + b_ref[...]


def _adaln(c_pad, w, b):
    d, n = w.shape
    return pl.pallas_call(
        _adaln_kernel,
        grid=(n // ADA_TN,),
        in_specs=[pl.BlockSpec((8, d), lambda j: (0, 0)),
                  pl.BlockSpec((d, ADA_TN), lambda j: (0, j)),
                  pl.BlockSpec((1, ADA_TN), lambda j: (0, j))],
        out_specs=pl.BlockSpec((8, ADA_TN), lambda j: (0, j)),
        out_shape=jax.ShapeDtypeStruct((8, n), F32),
        compiler_params=_params("arbitrary"),
        name="adaln",
    )(c_pad, w, b.reshape(1, n))


def _ffn_kernel(alpha, x_ref, sh_ref, sc_ref, gt_ref, wg_ref, wu_ref, wd_ref,
                lg_ref, lb_ref, o_ref, h_ref):
    j = pl.program_id(2)

    @pl.when(j == 0)
    def _():
        h_ref[...] = (x_ref[...] * (1.0 + sc_ref[...]) + sh_ref[...]).astype(BF16)
        o_ref[...] = jnp.zeros_like(o_ref)

    h = h_ref[...]
    g = jnp.dot(h, wg_ref[...], preferred_element_type=F32)
    u = jnp.dot(h, wu_ref[...], preferred_element_type=F32)
    a = (g / (1.0 + jnp.exp(-g)) * u).astype(BF16)
    o_ref[...] += jnp.dot(a, wd_ref[...], preferred_element_type=F32)

    @pl.when(j == pl.num_programs(2) - 1)
    def _():
        z = alpha * x_ref[...] + (0.5 * (1.0 + gt_ref[...])) * o_ref[...]
        o_ref[...] = _layer_norm_rows(z, lg_ref[...], lb_ref[...])


def _ffn_ln(x, shift, scale, gate, w_gu, w_down, ln_g, ln_b, alpha):
    bsz, s, d = x.shape
    f = w_down.shape[0]
    nf = f // FFN_TF
    vec = pl.BlockSpec((None, 1, d), lambda b, i, j: (b, 0, 0))
    row = pl.BlockSpec((1, d), lambda b, i, j: (0, 0))
    tile = pl.BlockSpec((None, FFN_TM, d), lambda b, i, j: (b, i, 0))
    return pl.pallas_call(
        functools.partial(_ffn_kernel, alpha),
        grid=(bsz, s // FFN_TM, nf),
        in_specs=[tile, vec, vec, vec,
                  pl.BlockSpec((d, FFN_TF), lambda b, i, j: (0, j)),
                  pl.BlockSpec((d, FFN_TF), lambda b, i, j: (0, nf + j)),
                  pl.BlockSpec((FFN_TF, d), lambda b, i, j: (j, 0)),
                  row, row],
        out_specs=tile,
        out_shape=jax.ShapeDtypeStruct((bsz, s, d), F32),
        scratch_shapes=[pltpu.VMEM((FFN_TM, d), BF16)],
        compiler_params=_params("arbitrary", "arbitrary", "arbitrary"),
        name="ffn_ln",
    )(x, shift, scale, gate, w_gu, w_gu, w_down, ln_g.reshape(1, d), ln_b.reshape(1, d))


def _gmlp_kernel(x_ref, sh_ref, sc_ref, wu_ref, wv_ref, lg_ref, lb_ref, ws_ref,
                 bst_ref, o_ref):
    pi = lax.broadcasted_iota(jnp.int32, (GMLP_BLOCK, GMLP_BLOCK), 0) // CHUNK
    pj = lax.broadcasted_iota(jnp.int32, (GMLP_BLOCK, GMLP_BLOCK), 1) // CHUNK
    keep = pj <= pi
    bst = bst_ref[...]

    h = (x_ref[...] * (1.0 + sc_ref[...]) + sh_ref[...]).astype(BF16)
    u = jax.nn.gelu(jnp.dot(h, wu_ref[...], preferred_element_type=F32))
    v = jax.nn.gelu(jnp.dot(h, wv_ref[...], preferred_element_type=F32))
    v = _layer_norm_rows(v, lg_ref[...], lb_ref[...]).astype(BF16)
    for g in range(GMLP_GROUPS):
        w = jnp.where(keep, ws_ref[g], 0.0).astype(BF16)
        bias = bst[:, g:g + 1]
        cols = slice(g * LANES, (g + 1) * LANES)
        for blk in range(x_ref.shape[0] // GMLP_BLOCK):
            rows = slice(blk * GMLP_BLOCK, (blk + 1) * GMLP_BLOCK)
            mixed = jnp.dot(w, v[rows, cols], preferred_element_type=F32) + bias
            o_ref[rows, cols] = (u[rows, cols] * mixed).astype(BF16)


def _gmlp(x, shift, scale, w_u, w_v, ln_g, ln_b, w_s, b_s_t):
    bsz, s, d = x.shape
    width = w_u.shape[1]
    vec = pl.BlockSpec((None, 1, d), lambda b, i: (b, 0, 0))
    full = lambda shape: pl.BlockSpec(shape, lambda b, i: (0,) * len(shape))
    return pl.pallas_call(
        _gmlp_kernel,
        grid=(bsz, s // MIX_TM),
        in_specs=[pl.BlockSpec((None, MIX_TM, d), lambda b, i: (b, i, 0)), vec, vec,
                  full((d, width)), full((d, width)),
                  full((1, width)), full((1, width)),
                  full(w_s.shape), full(b_s_t.shape)],
        out_specs=pl.BlockSpec((None, MIX_TM, width), lambda b, i: (b, i, 0)),
        out_shape=jax.ShapeDtypeStruct((bsz, s, width), BF16),
        compiler_params=_params("arbitrary", "arbitrary"),
        name="gmlp",
    )(x, shift, scale, w_u, w_v, ln_g.reshape(1, width), ln_b.reshape(1, width),
      w_s, b_s_t)


def _rope(x, cos, sin_signed, first_half):
    rot = jnp.where(first_half, pltpu.roll(x, 3 * LANES // 4, 1),
                    pltpu.roll(x, LANES // 4, 1))
    return x * cos + rot * sin_signed


def _qkv_kernel(x_ref, sh_ref, sc_ref, wq_ref, wk_ref, wv_ref, cos_ref, sin_ref,
                qq_ref, k_ref, vt_ref):
    t = x_ref.shape[0]
    h = (x_ref[...] * (1.0 + sc_ref[...]) + sh_ref[...]).astype(BF16)
    q = jnp.dot(h, wq_ref[...], preferred_element_type=F32)
    k = jnp.dot(h, wk_ref[...], preferred_element_type=F32)
    v = jnp.dot(h, wv_ref[...], preferred_element_type=F32)
    cos = cos_ref[...]
    sin = sin_ref[...]
    lane = lax.broadcasted_iota(jnp.int32, (t, LANES), 1)
    first_half = (lane % DIFF_QK_DIM) < (DIFF_QK_DIM // 2)
    comp = lax.broadcasted_iota(jnp.int32, (LANES, t), 0) < DIFF_QK_DIM
    q_scale = DIFF_QK_DIM ** -0.5 * LOG2_E
    ones = jnp.ones((V_AUG_ROWS - DIFF_V_DIM, t), BF16)
    row_chunk = lax.broadcasted_iota(jnp.int32, (t, LANES), 0) // CHUNK
    chunk_onehot = jnp.where(row_chunk == lane, 1.0, 0.0).astype(BF16)
    for hd in range(DIFF_HEADS):
        cols = slice(hd * LANES, (hd + 1) * LANES)
        k_ref[hd, :, :LANES] = _rope(k[:, cols], cos, sin, first_half).astype(BF16)
        k_ref[hd, :, LANES:] = chunk_onehot
        qt = (_rope(q[:, cols], cos, sin, first_half) * q_scale).T
        qq_ref[hd, 0, :, :t] = jnp.where(comp, qt, 0.0).astype(BF16)
        qq_ref[hd, 0, :, t:] = jnp.where(comp, 0.0, qt).astype(BF16)
        vt_ref[hd, 0, :DIFF_V_DIM, :] = v[:, cols].T.astype(BF16)
        vt_ref[hd, 0, DIFF_V_DIM:, :] = ones


def _qkv_rope(x, shift, scale, w_q, w_k, w_v, cos, sin_signed):
    bsz, s, d = x.shape
    t = ATT_T
    nt = s // t
    width = w_q.shape[1]
    vec = pl.BlockSpec((None, 1, d), lambda b, i: (b, 0, 0))
    wspec = pl.BlockSpec((d, width), lambda b, i: (0, 0))
    tab = pl.BlockSpec((t, LANES), lambda b, i: (i, 0))
    return pl.pallas_call(
        _qkv_kernel,
        grid=(bsz, nt),
        in_specs=[pl.BlockSpec((None, t, d), lambda b, i: (b, i, 0)), vec, vec,
                  wspec, wspec, wspec, tab, tab],
        out_specs=[
            pl.BlockSpec((None, DIFF_HEADS, 1, LANES, 2 * t), lambda b, i: (b, 0, i, 0, 0)),
            pl.BlockSpec((None, DIFF_HEADS, t, 2 * LANES), lambda b, i: (b, 0, i, 0)),
            pl.BlockSpec((None, DIFF_HEADS, 1, V_AUG_ROWS, t), lambda b, i: (b, 0, i, 0, 0)),
        ],
        out_shape=[
            jax.ShapeDtypeStruct((bsz, DIFF_HEADS, nt, LANES, 2 * t), BF16),
            jax.ShapeDtypeStruct((bsz, DIFF_HEADS, s, 2 * LANES), BF16),
            jax.ShapeDtypeStruct((bsz, DIFF_HEADS, nt, V_AUG_ROWS, t), BF16),
        ],
        compiler_params=_params("arbitrary", "arbitrary"),
        name="qkv_rope",
    )(x, shift, scale, w_q, w_k, w_v, cos, sin_signed)


def _attn_kernel(lam_init, n_steps, sched_ref, qq_ref, k_ref, vt_ref, bias_ref,
                 lq1_ref, lk1_ref, lq2_ref, lk2_ref, sg_ref, o_ref,
                 m_ref, acc_ref, fin_ref, s0_ref, s1_ref):
    t = ATT_T
    m_ref[...] = jnp.full_like(m_ref, NEG_INF)
    acc_ref[...] = jnp.zeros_like(acc_ref)

    def produce(n, s_ref):
        qi, kj, dg = sched_ref[0, n], sched_ref[1, n], sched_ref[2, n]
        rhs = jnp.concatenate([qq_ref[qi], bias_ref[dg]], axis=0)
        kblk = k_ref[pl.ds(pl.multiple_of(kj * t, t), t), :]
        s_ref[...] = jnp.dot(kblk, rhs, preferred_element_type=F32)

    def consume(n, s_ref):
        qi, kj = sched_ref[0, n], sched_ref[1, n]
        s = s_ref[...]
        m_prev = jnp.where(kj == 0, NEG_INF, m_ref[...])
        m_new = jnp.maximum(m_prev, jnp.max(s, axis=0, keepdims=True))
        alpha = jnp.exp2(m_prev - m_new)
        p = jnp.exp2(s - m_new).astype(BF16)
        pv = jnp.dot(vt_ref[kj], p, preferred_element_type=F32)
        acc = alpha * acc_ref[...] + pv
        acc_ref[...] = acc
        fin_ref[qi] = acc
        m_ref[...] = m_new

    produce(0, s0_ref)
    produce(1, s1_ref)

    def trip(g, carry):
        for u in range(ATT_UNROLL):
            n = g * ATT_UNROLL + u
            cur = s0_ref if u % 2 == 0 else s1_ref
            consume(n, cur)
            produce(n + 2, cur)
        return carry

    lax.fori_loop(0, n_steps // ATT_UNROLL, trip, 0)

    lam = (jnp.exp(jnp.sum(lq1_ref[...] * lk1_ref[...], keepdims=True))
           - jnp.exp(jnp.sum(lq2_ref[...] * lk2_ref[...], keepdims=True))
           + lam_init)
    sub_g = sg_ref[...] * (1.0 - lam_init)

    def finalize(qi, carry):
        acc = fin_ref[qi]
        o = acc[:DIFF_V_DIM, :] / acc[DIFF_V_DIM:DIFF_V_DIM + 1, :]
        o = o[:, :t] - lam * o[:, t:]
        ms = jnp.mean(o * o, axis=0, keepdims=True)
        o = o * lax.rsqrt(ms + LN_EPS) * sub_g
        o_ref[pl.ds(pl.multiple_of(qi * t, t), t), :] = o.T.astype(BF16)
        return carry

    lax.fori_loop(0, fin_ref.shape[0], finalize, 0)


def _attn_schedule(nt):
    steps = [(qi, kj, int(kj == qi)) for qi in range(nt) for kj in range(qi + 1)]
    n_steps = len(steps)
    steps += [(0, 0, 1)] * 2
    return jnp.asarray(steps, jnp.int32).T, n_steps


def _attn_mask_bias(t):
    key_chunk = jnp.arange(LANES)[:, None]
    q_chunk = (jnp.arange(2 * t)[None, :] % t) // CHUNK
    hidden = (key_chunk < t // CHUNK) & (key_chunk > q_chunk)
    diag = jnp.where(hidden, NEG_INF, 0.0)
    return jnp.stack([jnp.zeros_like(diag), diag]).astype(BF16)


def _diff_attn(qq, k, vt, lq1, lk1, lq2, lk2, sub_g, lam_init):
    bsz, heads, nt, _, _ = qq.shape
    t = ATT_T
    s = nt * t
    sched, n_steps = _attn_schedule(nt)
    assert n_steps % ATT_UNROLL == 0 and ATT_UNROLL % 2 == 0
    lam_spec = pl.BlockSpec((1, DIFF_QK_DIM), lambda b, h, sc: (0, 0))
    grid_spec = pltpu.PrefetchScalarGridSpec(
        num_scalar_prefetch=1,
        grid=(bsz, heads),
        in_specs=[
            pl.BlockSpec((None, None, nt, LANES, 2 * t), lambda b, h, sc: (b, h, 0, 0, 0)),
            pl.BlockSpec((None, None, s, 2 * LANES), lambda b, h, sc: (b, h, 0, 0)),
            pl.BlockSpec((None, None, nt, V_AUG_ROWS, t), lambda b, h, sc: (b, h, 0, 0, 0)),
            pl.BlockSpec((2, LANES, 2 * t), lambda b, h, sc: (0, 0, 0)),
            lam_spec, lam_spec, lam_spec, lam_spec,
            pl.BlockSpec((DIFF_V_DIM, 1), lambda b, h, sc: (0, 0)),
        ],
        out_specs=pl.BlockSpec((None, s, LANES), lambda b, h, sc: (b, 0, h)),
        scratch_shapes=[pltpu.VMEM((1, 2 * t), F32),
                        pltpu.VMEM((V_AUG_ROWS, 2 * t), F32),
                        pltpu.VMEM((nt, V_AUG_ROWS, 2 * t), F32),
                        pltpu.VMEM((t, 2 * t), F32), pltpu.VMEM((t, 2 * t), F32)],
    )
    return pl.pallas_call(
        functools.partial(_attn_kernel, lam_init, n_steps),
        grid_spec=grid_spec,
        out_shape=jax.ShapeDtypeStruct((bsz, s, heads * DIFF_V_DIM), BF16),
        compiler_params=_params("arbitrary", "arbitrary"),
        name="diff_attn",
    )(sched, qq, k, vt, _attn_mask_bias(t), lq1.reshape(1, -1), lk1.reshape(1, -1),
      lq2.reshape(1, -1), lk2.reshape(1, -1), sub_g.reshape(-1, 1))


def _out_kernel(alpha, x_ref, gt_ref, a_ref, b_ref, wa_ref, wb_ref, lg_ref, lb_ref,
                o_ref):
    gate = 1.0 + gt_ref[...]
    for c in range(x_ref.shape[0] // ROW_CHUNK):
        rows = pl.ds(c * ROW_CHUNK, ROW_CHUNK)
        y = jnp.dot(a_ref[rows, :], wa_ref[...], preferred_element_type=F32)
        y = y + jnp.dot(b_ref[rows, :], wb_ref[...], preferred_element_type=F32)
        z = alpha * x_ref[rows, :] + gate * y
        o_ref[rows, :] = _layer_norm_rows(z, lg_ref[...], lb_ref[...])


def _out_ln(x, gate, out_a, out_b, w_out, ln_g, ln_b, alpha):
    bsz, s, d = x.shape
    half = out_a.shape[-1]
    tile = pl.BlockSpec((None, MIX_TM, d), lambda b, i: (b, i, 0))
    act = pl.BlockSpec((None, MIX_TM, half), lambda b, i: (b, i, 0))
    row = pl.BlockSpec((1, d), lambda b, i: (0, 0))
    return pl.pallas_call(
        functools.partial(_out_kernel, alpha),
        grid=(bsz, s // MIX_TM),
        in_specs=[tile, pl.BlockSpec((None, 1, d), lambda b, i: (b, 0, 0)), act, act,
                  pl.BlockSpec((half, d), lambda b, i: (0, 0)),
                  pl.BlockSpec((half, d), lambda b, i: (1, 0)),
                  row, row],
        out_specs=tile,
        out_shape=jax.ShapeDtypeStruct((bsz, s, d), F32),
        compiler_params=_params("arbitrary", "arbitrary"),
        name="out_ln",
    )(x, gate, out_a, out_b, w_out, w_out, ln_g.reshape(1, d), ln_b.reshape(1, d))


def _rope_tables(s):
    half = DIFF_QK_DIM // 2
    pos = jnp.arange(s, dtype=F32)
    inv_freq = ROPE_THETA ** (-jnp.arange(0, DIFF_QK_DIM, 2, dtype=F32) / DIFF_QK_DIM)
    ang = pos[:, None] * inv_freq[None, :]
    cos, sin = jnp.cos(ang), jnp.sin(ang)
    cos = jnp.concatenate([cos, cos, cos, cos], axis=-1)
    sin_signed = jnp.concatenate([-sin, sin, -sin, sin], axis=-1)
    assert cos.shape[-1] == 4 * half == LANES
    return cos, sin_signed


def kernel(x, c, w_ada, b_ada, ffn1_w_gu, ffn1_w_down, ln1_g, ln1_b, w_in, gmlp_ln_g, gmlp_ln_b, gmlp_w_s, gmlp_b_s, lambda_q1, lambda_k1, lambda_q2, lambda_k2, diff_subln_g, w_out, ln2_g, ln2_b, ffn2_w_gu, ffn2_w_down, ln3_g, ln3_b):
    bsz, s, d = x.shape
    depth = w_ada.shape[0]
    alpha = (2 * depth) ** 0.25
    gw = gmlp_ln_g.shape[-1]
    qk = DIFF_HEADS * 2 * DIFF_QK_DIM
    c1, c2, c3, c4 = gw, 2 * gw, 2 * gw + qk, 2 * gw + 2 * qk
    cos, sin_signed = _rope_tables(s)
    c_pad = jnp.zeros((8, d), F32).at[:bsz].set(c)

    for l in range(depth):
        mod = _adaln(c_pad, w_ada[l], b_ada[l])[:bsz]
        mod = mod.reshape(bsz, N_MOD, 1, d)
        s1, sc1, g1, s2, sc2, g2, s3, sc3, g3 = [mod[:, n] for n in range(N_MOD)]

        x = _ffn_ln(x, s1, sc1, g1, ffn1_w_gu[l].astype(BF16),
                    ffn1_w_down[l].astype(BF16), ln1_g[l], ln1_b[l], alpha)

        w_in_b = w_in[l].astype(BF16)
        out_a = _gmlp(x, s2, sc2, w_in_b[:, :c1], w_in_b[:, c1:c2], gmlp_ln_g[l],
                      gmlp_ln_b[l], gmlp_w_s[l], jnp.transpose(gmlp_b_s[l]))
        qq, k, vt = _qkv_rope(x, s2, sc2, w_in_b[:, c2:c3], w_in_b[:, c3:c4],
                              w_in_b[:, c4:], cos, sin_signed)
        lam_init = 0.8 - 0.6 * math.exp(-0.3 * l)
        out_b = _diff_attn(qq, k, vt, lambda_q1[l], lambda_k1[l], lambda_q2[l],
                           lambda_k2[l], diff_subln_g[l], lam_init)
        x = _out_ln(x, g2, out_a, out_b, w_out[l].astype(BF16), ln2_g[l], ln2_b[l], alpha)

        x = _ffn_ln(x, s3, sc3, g3, ffn2_w_gu[l].astype(BF16),
                    ffn2_w_down[l].astype(BF16), ln3_g[l], ln3_b[l], alpha)
    return x
```

```python
import functools
import math

import jax
import jax.numpy as jnp
from jax import lax
from jax.experimental import pallas as pl
from jax.experimental.pallas import tpu as pltpu

F32 = jnp.float32
BF16 = jnp.bfloat16

CHUNK = 64
GMLP_BLOCK = 128
GMLP_GROUPS = 8
DIFF_HEADS = 8
DIFF_V_DIM = 128
DIFF_QK_DIM = 64
N_MOD = 9
ROPE_THETA = 10000.0
LN_EPS = 1e-5
NEG_INF = -1e30
LOG2_E = 1.4426950408889634
V_AUG_ROWS = DIFF_V_DIM + 16

LANES = 128
VMEM_LIMIT_BYTES = 56 * 1024 * 1024
FFN_VMEM_LIMIT_BYTES = 61 * 1024 * 1024

ADA_TN = 1024
FFN_TM = 1024
FFN_TF = 512
FFN_UP_COLS = 256
FFN_DOWN_COLS = 256
MIX_TM = 512
ROW_CHUNK = 256
ATT_T = 512
ATT_UNROLL = 34


def _params(*sem, vmem_limit_bytes=VMEM_LIMIT_BYTES):
    return pltpu.CompilerParams(dimension_semantics=sem,
                                vmem_limit_bytes=vmem_limit_bytes)


def _layer_norm_rows(z, g, b):
    mu = jnp.mean(z, axis=-1, keepdims=True)
    zc = z - mu
    var = jnp.mean(zc * zc, axis=-1, keepdims=True)
    return zc * lax.rsqrt(var + LN_EPS) * g + b


def _adaln_kernel(c_ref, w_ref, b_ref, o_ref):
    c = c_ref[...]
    ca = (c / (1.0 + jnp.exp(-c))).astype(BF16)
    o_ref[...] = jnp.dot(ca, w_ref[...].astype(BF16),
                         preferred_element_type=F32) + b_ref[...]


def _adaln(c_pad, w, b):
    d, n = w.shape
    return pl.pallas_call(
        _adaln_kernel,
        grid=(n // ADA_TN,),
        in_specs=[pl.BlockSpec((8, d), lambda j: (0, 0)),
                  pl.BlockSpec((d, ADA_TN), lambda j: (0, j)),
                  pl.BlockSpec((1, ADA_TN), lambda j: (0, j))],
        out_specs=pl.BlockSpec((8, ADA_TN), lambda j: (0, j)),
        out_shape=jax.ShapeDtypeStruct((8, n), F32),
        compiler_params=_params("arbitrary"),
        name="adaln",
    )(c_pad, w, b.reshape(1, n))


def _ffn_kernel(alpha, x_ref, sh_ref, sc_ref, gt_ref, wg_ref, wu_ref, wd_ref,
                lg_ref, lb_ref, o_ref, h_ref, a_ref):
    j = pl.program_id(2)

    @pl.when(j == 0)
    def _():
        h_ref[...] = (x_ref[...] * (1.0 + sc_ref[...]) + sh_ref[...]).astype(BF16)
        o_ref[...] = jnp.zeros_like(o_ref)

    h = h_ref[...]
    for c in range(a_ref.shape[1] // FFN_UP_COLS):
        cols = slice(c * FFN_UP_COLS, (c + 1) * FFN_UP_COLS)
        g = jnp.dot(h, wg_ref[:, cols], preferred_element_type=F32)
        u = jnp.dot(h, wu_ref[:, cols], preferred_element_type=F32)
        a_ref[:, cols] = (g / (1.0 + jnp.exp(-g)) * u).astype(BF16)
    a = a_ref[...]
    for c in range(o_ref.shape[1] // FFN_DOWN_COLS):
        cols = slice(c * FFN_DOWN_COLS, (c + 1) * FFN_DOWN_COLS)
        o_ref[:, cols] += jnp.dot(a, wd_ref[:, cols], preferred_element_type=F32)

    @pl.when(j == pl.num_programs(2) - 1)
    def _():
        gate = 0.5 * (1.0 + gt_ref[...])
        for r in range(o_ref.shape[0] // ROW_CHUNK):
            rows = pl.ds(r * ROW_CHUNK, ROW_CHUNK)
            z = alpha * x_ref[rows, :] + gate * o_ref[rows, :]
            o_ref[rows, :] = _layer_norm_rows(z, lg_ref[...], lb_ref[...])


def _ffn_ln(x, shift, scale, gate, w_gu, w_down, ln_g, ln_b, alpha):
    bsz, s, d = x.shape
    f = w_down.shape[0]
    nf = f // FFN_TF
    vec = pl.BlockSpec((None, 1, d), lambda b, i, j: (b, 0, 0))
    row = pl.BlockSpec((1, d), lambda b, i, j: (0, 0))
    tile = pl.BlockSpec((None, FFN_TM, d), lambda b, i, j: (b, i, 0))
    return pl.pallas_call(
        functools.partial(_ffn_kernel, alpha),
        grid=(bsz, s // FFN_TM, nf),
        in_specs=[tile, vec, vec, vec,
                  pl.BlockSpec((d, FFN_TF), lambda b, i, j: (0, j)),
                  pl.BlockSpec((d, FFN_TF), lambda b, i, j: (0, nf + j)),
                  pl.BlockSpec((FFN_TF, d), lambda b, i, j: (j, 0)),
                  row, row],
        out_specs=tile,
        out_shape=jax.ShapeDtypeStruct((bsz, s, d), F32),
        scratch_shapes=[pltpu.VMEM((FFN_TM, d), BF16), pltpu.VMEM((FFN_TM, FFN_TF), BF16)],
        compiler_params=_params("arbitrary", "arbitrary", "arbitrary",
                                vmem_limit_bytes=FFN_VMEM_LIMIT_BYTES),
        name="ffn_ln",
    )(x, shift, scale, gate, w_gu, w_gu, w_down, ln_g.reshape(1, d), ln_b.reshape(1, d))


def _gmlp_kernel(x_ref, sh_ref, sc_ref, wu_ref, wv_ref, lg_ref, lb_ref, ws_ref,
                 bst_ref, o_ref):
    pi = lax.broadcasted_iota(jnp.int32, (GMLP_BLOCK, GMLP_BLOCK), 0) // CHUNK
    pj = lax.broadcasted_iota(jnp.int32, (GMLP_BLOCK, GMLP_BLOCK), 1) // CHUNK
    keep = pj <= pi
    bst = bst_ref[...]

    h = (x_ref[...] * (1.0 + sc_ref[...]) + sh_ref[...]).astype(BF16)
    u = jax.nn.gelu(jnp.dot(h, wu_ref[...], preferred_element_type=F32))
    v = jax.nn.gelu(jnp.dot(h, wv_ref[...], preferred_element_type=F32))
    v = _layer_norm_rows(v, lg_ref[...], lb_ref[...]).astype(BF16)
    for g in range(GMLP_GROUPS):
        w = jnp.where(keep, ws_ref[g], 0.0).astype(BF16)
        bias = bst[:, g:g + 1]
        cols = slice(g * LANES, (g + 1) * LANES)
        for blk in range(x_ref.shape[0] // GMLP_BLOCK):
            rows = slice(blk * GMLP_BLOCK, (blk + 1) * GMLP_BLOCK)
            mixed = jnp.dot(w, v[rows, cols], preferred_element_type=F32) + bias
            o_ref[rows, cols] = (u[rows, cols] * mixed).astype(BF16)


def _gmlp(x, shift, scale, w_u, w_v, ln_g, ln_b, w_s, b_s_t):
    bsz, s, d = x.shape
    width = w_u.shape[1]
    vec = pl.BlockSpec((None, 1, d), lambda b, i: (b, 0, 0))
    full = lambda shape: pl.BlockSpec(shape, lambda b, i: (0,) * len(shape))
    return pl.pallas_call(
        _gmlp_kernel,
        grid=(bsz, s // MIX_TM),
        in_specs=[pl.BlockSpec((None, MIX_TM, d), lambda b, i: (b, i, 0)), vec, vec,
                  full((d, width)), full((d, width)),
                  full((1, width)), full((1, width)),
                  full(w_s.shape), full(b_s_t.shape)],
        out_specs=pl.BlockSpec((None, MIX_TM, width), lambda b, i: (b, i, 0)),
        out_shape=jax.ShapeDtypeStruct((bsz, s, width), BF16),
        compiler_params=_params("arbitrary", "arbitrary"),
        name="gmlp",
    )(x, shift, scale, w_u, w_v, ln_g.reshape(1, width), ln_b.reshape(1, width),
      w_s, b_s_t)


def _rope(x, cos, sin_signed, first_half):
    rot = jnp.where(first_half, pltpu.roll(x, 3 * LANES // 4, 1),
                    pltpu.roll(x, LANES // 4, 1))
    return x * cos + rot * sin_signed


def _qkv_kernel(x_ref, sh_ref, sc_ref, wq_ref, wk_ref, wv_ref, cos_ref, sin_ref,
                qq_ref, k_ref, vt_ref):
    t = x_ref.shape[0]
    h = (x_ref[...] * (1.0 + sc_ref[...]) + sh_ref[...]).astype(BF16)
    q = jnp.dot(h, wq_ref[...], preferred_element_type=F32)
    k = jnp.dot(h, wk_ref[...], preferred_element_type=F32)
    v = jnp.dot(h, wv_ref[...], preferred_element_type=F32)
    cos = cos_ref[...]
    sin = sin_ref[...]
    lane = lax.broadcasted_iota(jnp.int32, (t, LANES), 1)
    first_half = (lane % DIFF_QK_DIM) < (DIFF_QK_DIM // 2)
    comp = lax.broadcasted_iota(jnp.int32, (LANES, t), 0) < DIFF_QK_DIM
    q_scale = DIFF_QK_DIM ** -0.5 * LOG2_E
    ones = jnp.ones((V_AUG_ROWS - DIFF_V_DIM, t), BF16)
    row_chunk = lax.broadcasted_iota(jnp.int32, (t, LANES), 0) // CHUNK
    chunk_onehot = jnp.where(row_chunk == lane, 1.0, 0.0).astype(BF16)
    for hd in range(DIFF_HEADS):
        cols = slice(hd * LANES, (hd + 1) * LANES)
        k_ref[hd, :, :LANES] = _rope(k[:, cols], cos, sin, first_half).astype(BF16)
        k_ref[hd, :, LANES:] = chunk_onehot
        qt = (_rope(q[:, cols], cos, sin, first_half) * q_scale).T
        qq_ref[hd, 0, :, :t] = jnp.where(comp, qt, 0.0).astype(BF16)
        qq_ref[hd, 0, :, t:] = jnp.where(comp, 0.0, qt).astype(BF16)
        vt_ref[hd, 0, :DIFF_V_DIM, :] = v[:, cols].T.astype(BF16)
        vt_ref[hd, 0, DIFF_V_DIM:, :] = ones


def _qkv_rope(x, shift, scale, w_q, w_k, w_v, cos, sin_signed):
    bsz, s, d = x.shape
    t = ATT_T
    nt = s // t
    width = w_q.shape[1]
    vec = pl.BlockSpec((None, 1, d), lambda b, i: (b, 0, 0))
    wspec = pl.BlockSpec((d, width), lambda b, i: (0, 0))
    tab = pl.BlockSpec((t, LANES), lambda b, i: (i, 0))
    return pl.pallas_call(
        _qkv_kernel,
        grid=(bsz, nt),
        in_specs=[pl.BlockSpec((None, t, d), lambda b, i: (b, i, 0)), vec, vec,
                  wspec, wspec, wspec, tab, tab],
        out_specs=[
            pl.BlockSpec((None, DIFF_HEADS, 1, LANES, 2 * t), lambda b, i: (b, 0, i, 0, 0)),
            pl.BlockSpec((None, DIFF_HEADS, t, 2 * LANES), lambda b, i: (b, 0, i, 0)),
            pl.BlockSpec((None, DIFF_HEADS, 1, V_AUG_ROWS, t), lambda b, i: (b, 0, i, 0, 0)),
        ],
        out_shape=[
            jax.ShapeDtypeStruct((bsz, DIFF_HEADS, nt, LANES, 2 * t), BF16),
            jax.ShapeDtypeStruct((bsz, DIFF_HEADS, s, 2 * LANES), BF16),
            jax.ShapeDtypeStruct((bsz, DIFF_HEADS, nt, V_AUG_ROWS, t), BF16),
        ],
        compiler_params=_params("arbitrary", "arbitrary"),
        name="qkv_rope",
    )(x, shift, scale, w_q, w_k, w_v, cos, sin_signed)


def _attn_kernel(lam_init, n_steps, sched_ref, qq_ref, k_ref, vt_ref, bias_ref,
                 lq1_ref, lk1_ref, lq2_ref, lk2_ref, sg_ref, o_ref,
                 m_ref, acc_ref, fin_ref, s0_ref, s1_ref):
    t = ATT_T
    m_ref[...] = jnp.full_like(m_ref, NEG_INF)
    acc_ref[...] = jnp.zeros_like(acc_ref)

    def produce(n, s_ref):
        qi, kj, dg = sched_ref[0, n], sched_ref[1, n], sched_ref[2, n]
        rhs = jnp.concatenate([qq_ref[qi], bias_ref[dg]], axis=0)
        kblk = k_ref[pl.ds(pl.multiple_of(kj * t, t), t), :]
        s_ref[...] = jnp.dot(kblk, rhs, preferred_element_type=F32)

    def consume(n, s_ref):
        qi, kj = sched_ref[0, n], sched_ref[1, n]
        s = s_ref[...]
        m_prev = jnp.where(kj == 0, NEG_INF, m_ref[...])
        m_new = jnp.maximum(m_prev, jnp.max(s, axis=0, keepdims=True))
        alpha = jnp.exp2(m_prev - m_new)
        p = jnp.exp2(s - m_new).astype(BF16)
        pv = jnp.dot(vt_ref[kj], p, preferred_element_type=F32)
        acc = alpha * acc_ref[...] + pv
        acc_ref[...] = acc
        fin_ref[qi] = acc
        m_ref[...] = m_new

    produce(0, s0_ref)
    produce(1, s1_ref)

    def trip(g, carry):
        for u in range(ATT_UNROLL):
            n = g * ATT_UNROLL + u
            cur = s0_ref if u % 2 == 0 else s1_ref
            consume(n, cur)
            produce(n + 2, cur)
        return carry

    lax.fori_loop(0, n_steps // ATT_UNROLL, trip, 0)

    lam = (jnp.exp(jnp.sum(lq1_ref[...] * lk1_ref[...], keepdims=True))
           - jnp.exp(jnp.sum(lq2_ref[...] * lk2_ref[...], keepdims=True))
           + lam_init)
    sub_g = sg_ref[...] * (1.0 - lam_init)

    def finalize(qi, carry):
        acc = fin_ref[qi]
        o = acc[:DIFF_V_DIM, :] / acc[DIFF_V_DIM:DIFF_V_DIM + 1, :]
        o = o[:, :t] - lam * o[:, t:]
        ms = jnp.mean(o * o, axis=0, keepdims=True)
        o = o * lax.rsqrt(ms + LN_EPS) * sub_g
        o_ref[pl.ds(pl.multiple_of(qi * t, t), t), :] = o.T.astype(BF16)
        return carry

    lax.fori_loop(0, fin_ref.shape[0], finalize, 0)


def _attn_schedule(nt):
    steps = [(qi, kj, int(kj == qi)) for qi in range(nt) for kj in range(qi + 1)]
    n_steps = len(steps)
    steps += [(0, 0, 1)] * 2
    return jnp.asarray(steps, jnp.int32).T, n_steps


def _attn_mask_bias(t):
    key_chunk = jnp.arange(LANES)[:, None]
    q_chunk = (jnp.arange(2 * t)[None, :] % t) // CHUNK
    hidden = (key_chunk < t // CHUNK) & (key_chunk > q_chunk)
    diag = jnp.where(hidden, NEG_INF, 0.0)
    return jnp.stack([jnp.zeros_like(diag), diag]).astype(BF16)


def _diff_attn(qq, k, vt, lq1, lk1, lq2, lk2, sub_g, lam_init):
    bsz, heads, nt, _, _ = qq.shape
    t = ATT_T
    s = nt * t
    sched, n_steps = _attn_schedule(nt)
    assert n_steps % ATT_UNROLL == 0 and ATT_UNROLL % 2 == 0
    lam_spec = pl.BlockSpec((1, DIFF_QK_DIM), lambda b, h, sc: (0, 0))
    grid_spec = pltpu.PrefetchScalarGridSpec(
        num_scalar_prefetch=1,
        grid=(bsz, heads),
        in_specs=[
            pl.BlockSpec((None, None, nt, LANES, 2 * t), lambda b, h, sc: (b, h, 0, 0, 0)),
            pl.BlockSpec((None, None, s, 2 * LANES), lambda b, h, sc: (b, h, 0, 0)),
            pl.BlockSpec((None, None, nt, V_AUG_ROWS, t), lambda b, h, sc: (b, h, 0, 0, 0)),
            pl.BlockSpec((2, LANES, 2 * t), lambda b, h, sc: (0, 0, 0)),
            lam_spec, lam_spec, lam_spec, lam_spec,
            pl.BlockSpec((DIFF_V_DIM, 1), lambda b, h, sc: (0, 0)),
        ],
        out_specs=pl.BlockSpec((None, s, LANES), lambda b, h, sc: (b, 0, h)),
        scratch_shapes=[pltpu.VMEM((1, 2 * t), F32),
                        pltpu.VMEM((V_AUG_ROWS, 2 * t), F32),
                        pltpu.VMEM((nt, V_AUG_ROWS, 2 * t), F32),
                        pltpu.VMEM((t, 2 * t), F32), pltpu.VMEM((t, 2 * t), F32)],
    )
    return pl.pallas_call(
        functools.partial(_attn_kernel, lam_init, n_steps),
        grid_spec=grid_spec,
        out_shape=jax.ShapeDtypeStruct((bsz, s, heads * DIFF_V_DIM), BF16),
        compiler_params=_params("arbitrary", "arbitrary"),
        name="diff_attn",
    )(sched, qq, k, vt, _attn_mask_bias(t), lq1.reshape(1, -1), lk1.reshape(1, -1),
      lq2.reshape(1, -1), lk2.reshape(1, -1), sub_g.reshape(-1, 1))


def _out_kernel(alpha, x_ref, gt_ref, a_ref, b_ref, wa_ref, wb_ref, lg_ref, lb_ref,
                o_ref):
    gate = 1.0 + gt_ref[...]
    for c in range(x_ref.shape[0] // ROW_CHUNK):
        rows = pl.ds(c * ROW_CHUNK, ROW_CHUNK)
        y = jnp.dot(a_ref[rows, :], wa_ref[...], preferred_element_type=F32)
        y = y + jnp.dot(b_ref[rows, :], wb_ref[...], preferred_element_type=F32)
        z = alpha * x_ref[rows, :] + gate * y
        o_ref[rows, :] = _layer_norm_rows(z, lg_ref[...], lb_ref[...])


def _out_ln(x, gate, out_a, out_b, w_out, ln_g, ln_b, alpha):
    bsz, s, d = x.shape
    half = out_a.shape[-1]
    tile = pl.BlockSpec((None, MIX_TM, d), lambda b, i: (b, i, 0))
    act = pl.BlockSpec((None, MIX_TM, half), lambda b, i: (b, i, 0))
    row = pl.BlockSpec((1, d), lambda b, i: (0, 0))
    return pl.pallas_call(
        functools.partial(_out_kernel, alpha),
        grid=(bsz, s // MIX_TM),
        in_specs=[tile, pl.BlockSpec((None, 1, d), lambda b, i: (b, 0, 0)), act, act,
                  pl.BlockSpec((half, d), lambda b, i: (0, 0)),
                  pl.BlockSpec((half, d), lambda b, i: (1, 0)),
                  row, row],
        out_specs=tile,
        out_shape=jax.ShapeDtypeStruct((bsz, s, d), F32),
        compiler_params=_params("arbitrary", "arbitrary"),
        name="out_ln",
    )(x, gate, out_a, out_b, w_out, w_out, ln_g.reshape(1, d), ln_b.reshape(1, d))


def _rope_tables(s):
    half = DIFF_QK_DIM // 2
    pos = jnp.arange(s, dtype=F32)
    inv_freq = ROPE_THETA ** (-jnp.arange(0, DIFF_QK_DIM, 2, dtype=F32) / DIFF_QK_DIM)
    ang = pos[:, None] * inv_freq[None, :]
    cos, sin = jnp.cos(ang), jnp.sin(ang)
    cos = jnp.concatenate([cos, cos, cos, cos], axis=-1)
    sin_signed = jnp.concatenate([-sin, sin, -sin, sin], axis=-1)
    assert cos.shape[-1] == 4 * half == LANES
    return cos, sin_signed


def kernel(x, c, w_ada, b_ada, ffn1_w_gu, ffn1_w_down, ln1_g, ln1_b, w_in, gmlp_ln_g, gmlp_ln_b, gmlp_w_s, gmlp_b_s, lambda_q1, lambda_k1, lambda_q2, lambda_k2, diff_subln_g, w_out, ln2_g, ln2_b, ffn2_w_gu, ffn2_w_down, ln3_g, ln3_b):
    bsz, s, d = x.shape
    depth = w_ada.shape[0]
    alpha = (2 * depth) ** 0.25
    gw = gmlp_ln_g.shape[-1]
    qk = DIFF_HEADS * 2 * DIFF_QK_DIM
    c1, c2, c3, c4 = gw, 2 * gw, 2 * gw + qk, 2 * gw + 2 * qk
    cos, sin_signed = _rope_tables(s)
    c_pad = jnp.zeros((8, d), F32).at[:bsz].set(c)

    for l in range(depth):
        mod = _adaln(c_pad, w_ada[l], b_ada[l])[:bsz]
        mod = mod.reshape(bsz, N_MOD, 1, d)
        s1, sc1, g1, s2, sc2, g2, s3, sc3, g3 = [mod[:, n] for n in range(N_MOD)]

        x = _ffn_ln(x, s1, sc1, g1, ffn1_w_gu[l].astype(BF16),
                    ffn1_w_down[l].astype(BF16), ln1_g[l], ln1_b[l], alpha)

        w_in_b = w_in[l].astype(BF16)
        out_a = _gmlp(x, s2, sc2, w_in_b[:, :c1], w_in_b[:, c1:c2], gmlp_ln_g[l],
                      gmlp_ln_b[l], gmlp_w_s[l], jnp.transpose(gmlp_b_s[l]))
        qq, k, vt = _qkv_rope(x, s2, sc2, w_in_b[:, c2:c3], w_in_b[:, c3:c4],
                              w_in_b[:, c4:], cos, sin_signed)
        lam_init = 0.8 - 0.6 * math.exp(-0.3 * l)
        out_b = _diff_attn(qq, k, vt, lambda_q1[l], lambda_k1[l], lambda_q2[l],
                           lambda_k2[l], diff_subln_g[l], lam_init)
        x = _out_ln(x, g2, out_a, out_b, w_out[l].astype(BF16), ln2_g[l], ln2_b[l], alpha)

        x = _ffn_ln(x, s3, sc3, g3, ffn2_w_gu[l].astype(BF16),
                    ffn2_w_down[l].astype(BF16), ln3_g[l], ln3_b[l], alpha)
    return x
```

```python
import functools
import math

import jax
import jax.numpy as jnp
from jax import lax
from jax.experimental import pallas as pl
from jax.experimental.pallas import tpu as pltpu

F32 = jnp.float32
BF16 = jnp.bfloat16

CHUNK = 64
GMLP_BLOCK = 128
GMLP_GROUPS = 8
DIFF_HEADS = 8
DIFF_V_DIM = 128
DIFF_QK_DIM = 64
N_MOD = 9
ROPE_THETA = 10000.0
LN_EPS = 1e-5
NEG_INF = -1e30
LOG2_E = 1.4426950408889634
V_AUG_ROWS = DIFF_V_DIM + 16

LANES = 128
VMEM_LIMIT_BYTES = 56 * 1024 * 1024
FFN_VMEM_LIMIT_BYTES = 61 * 1024 * 1024

ADA_TN = 1024
FFN_TM = 1024
FFN_TF = 512
FFN_UP_COLS = 256
FFN_DOWN_COLS = 256
MIX_TM = 512
ROW_CHUNK = 256
ATT_T = 512
ATT_UNROLL = 34


def _params(*sem, vmem_limit_bytes=VMEM_LIMIT_BYTES):
    return pltpu.CompilerParams(dimension_semantics=sem,
                                vmem_limit_bytes=vmem_limit_bytes)


def _layer_norm_rows(z, g, b):
    mu = jnp.mean(z, axis=-1, keepdims=True)
    zc = z - mu
    var = jnp.mean(zc * zc, axis=-1, keepdims=True)
    return zc * lax.rsqrt(var + LN_EPS) * g + b


def _adaln_kernel(c_ref, w_ref, b_ref, o_ref):
    c = c_ref[...]
    ca = (c / (1.0 + jnp.exp(-c))).astype(BF16)
    o_ref[...] = jnp.dot(ca, w_ref[...].astype(BF16),
                         preferred_element_type=F32) + b_ref[...]


def _adaln(c_pad, w, b):
    d, n = w.shape
    return pl.pallas_call(
        _adaln_kernel,
        grid=(n // ADA_TN,),
        in_specs=[pl.BlockSpec((8, d), lambda j: (0, 0)),
                  pl.BlockSpec((d, ADA_TN), lambda j: (0, j)),
                  pl.BlockSpec((1, ADA_TN), lambda j: (0, j))],
        out_specs=pl.BlockSpec((8, ADA_TN), lambda j: (0, j)),
        out_shape=jax.ShapeDtypeStruct((8, n), F32),
        compiler_params=_params("arbitrary"),
        name="adaln",
    )(c_pad, w, b.reshape(1, n))


def _ffn_kernel(alpha, x_ref, sh_ref, sc_ref, gt_ref, wg_ref, wu_ref, wd_ref,
                lg_ref, lb_ref, o_ref, h_ref, a_ref):
    j = pl.program_id(2)
    last = pl.num_programs(2) - 1

    def hidden():
        h = h_ref[...]
        for c in range(a_ref.shape[1] // FFN_UP_COLS):
            cols = slice(c * FFN_UP_COLS, (c + 1) * FFN_UP_COLS)
            g = jnp.dot(h, wg_ref[:, cols], preferred_element_type=F32)
            u = jnp.dot(h, wu_ref[:, cols], preferred_element_type=F32)
            a_ref[:, cols] = (g / (1.0 + jnp.exp(-g)) * u).astype(BF16)

    def down(first):
        a = a_ref[...]
        for c in range(o_ref.shape[1] // FFN_DOWN_COLS):
            cols = slice(c * FFN_DOWN_COLS, (c + 1) * FFN_DOWN_COLS)
            y = jnp.dot(a, wd_ref[:, cols], preferred_element_type=F32)
            o_ref[:, cols] = y if first else o_ref[:, cols] + y

    @pl.when(j == 0)
    def _():
        h_ref[...] = (x_ref[...] * (1.0 + sc_ref[...]) + sh_ref[...]).astype(BF16)
        hidden()
        down(first=True)

    @pl.when(jnp.logical_and(j > 0, j < last))
    def _():
        hidden()
        down(first=False)

    @pl.when(j == last)
    def _():
        hidden()
        gate = 0.5 * (1.0 + gt_ref[...])
        for r in range(o_ref.shape[0] // ROW_CHUNK):
            rows = pl.ds(r * ROW_CHUNK, ROW_CHUNK)
            y = o_ref[rows, :] + jnp.dot(a_ref[rows, :], wd_ref[...],
                                         preferred_element_type=F32)
            z = alpha * x_ref[rows, :] + gate * y
            o_ref[rows, :] = _layer_norm_rows(z, lg_ref[...], lb_ref[...])


def _ffn_ln(x, shift, scale, gate, w_gu, w_down, ln_g, ln_b, alpha):
    bsz, s, d = x.shape
    f = w_down.shape[0]
    nf = f // FFN_TF
    vec = pl.BlockSpec((None, 1, d), lambda b, i, j: (b, 0, 0))
    row = pl.BlockSpec((1, d), lambda b, i, j: (0, 0))
    tile = pl.BlockSpec((None, FFN_TM, d), lambda b, i, j: (b, i, 0))
    return pl.pallas_call(
        functools.partial(_ffn_kernel, alpha),
        grid=(bsz, s // FFN_TM, nf),
        in_specs=[tile, vec, vec, vec,
                  pl.BlockSpec((d, FFN_TF), lambda b, i, j: (0, j)),
                  pl.BlockSpec((d, FFN_TF), lambda b, i, j: (0, nf + j)),
                  pl.BlockSpec((FFN_TF, d), lambda b, i, j: (j, 0)),
                  row, row],
        out_specs=tile,
        out_shape=jax.ShapeDtypeStruct((bsz, s, d), F32),
        scratch_shapes=[pltpu.VMEM((FFN_TM, d), BF16), pltpu.VMEM((FFN_TM, FFN_TF), BF16)],
        compiler_params=_params("arbitrary", "arbitrary", "arbitrary",
                                vmem_limit_bytes=FFN_VMEM_LIMIT_BYTES),
        name="ffn_ln",
    )(x, shift, scale, gate, w_gu, w_gu, w_down, ln_g.reshape(1, d), ln_b.reshape(1, d))


def _gmlp_kernel(n_cast, x_ref, sh_ref, sc_ref, wu_ref, wv_ref, lg_ref, lb_ref, ws_ref,
                 bst_ref, *rest):
    cast_in, o_ref, cast_out = rest[:n_cast], rest[n_cast], rest[n_cast + 1:]
    for src, dst in zip(cast_in, cast_out):
        dst[...] = src[...].astype(BF16)

    pi = lax.broadcasted_iota(jnp.int32, (GMLP_BLOCK, GMLP_BLOCK), 0) // CHUNK
    pj = lax.broadcasted_iota(jnp.int32, (GMLP_BLOCK, GMLP_BLOCK), 1) // CHUNK
    keep = pj <= pi
    bst = bst_ref[...]

    h = (x_ref[...] * (1.0 + sc_ref[...]) + sh_ref[...]).astype(BF16)
    u = jax.nn.gelu(jnp.dot(h, wu_ref[...], preferred_element_type=F32))
    v = jax.nn.gelu(jnp.dot(h, wv_ref[...], preferred_element_type=F32))
    v = _layer_norm_rows(v, lg_ref[...], lb_ref[...]).astype(BF16)
    for g in range(GMLP_GROUPS):
        w = jnp.where(keep, ws_ref[g], 0.0).astype(BF16)
        bias = bst[:, g:g + 1]
        cols = slice(g * LANES, (g + 1) * LANES)
        for blk in range(x_ref.shape[0] // GMLP_BLOCK):
            rows = slice(blk * GMLP_BLOCK, (blk + 1) * GMLP_BLOCK)
            mixed = jnp.dot(w, v[rows, cols], preferred_element_type=F32) + bias
            o_ref[rows, cols] = (u[rows, cols] * mixed).astype(BF16)


def _gmlp(x, shift, scale, w_u, w_v, ln_g, ln_b, w_s, b_s_t, cast_weights):
    bsz, s, d = x.shape
    width = w_u.shape[1]
    nt = s // MIX_TM
    vec = pl.BlockSpec((None, 1, d), lambda b, i: (b, 0, 0))
    full = lambda shape: pl.BlockSpec(shape, lambda b, i: (0,) * len(shape))
    slab_specs = []
    for w in cast_weights:
        rows = w.shape[0] // (bsz * nt)
        assert rows * bsz * nt == w.shape[0] and rows % 16 == 0, w.shape
        slab_specs.append(pl.BlockSpec((rows, w.shape[1]), lambda b, i: (b * nt + i, 0)))
    return pl.pallas_call(
        functools.partial(_gmlp_kernel, len(cast_weights)),
        grid=(bsz, nt),
        in_specs=[pl.BlockSpec((None, MIX_TM, d), lambda b, i: (b, i, 0)), vec, vec,
                  full((d, width)), full((d, width)),
                  full((1, width)), full((1, width)),
                  full(w_s.shape), full(b_s_t.shape)] + slab_specs,
        out_specs=[pl.BlockSpec((None, MIX_TM, width), lambda b, i: (b, i, 0))] + slab_specs,
        out_shape=[jax.ShapeDtypeStruct((bsz, s, width), BF16)]
        + [jax.ShapeDtypeStruct(w.shape, BF16) for w in cast_weights],
        compiler_params=_params("arbitrary", "arbitrary"),
        name="gmlp",
    )(x, shift, scale, w_u, w_v, ln_g.reshape(1, width), ln_b.reshape(1, width),
      w_s, b_s_t, *cast_weights)


def _rope(x, cos, sin_signed, first_half):
    rot = jnp.where(first_half, pltpu.roll(x, 3 * LANES // 4, 1),
                    pltpu.roll(x, LANES // 4, 1))
    return x * cos + rot * sin_signed


def _qkv_kernel(x_ref, sh_ref, sc_ref, wq_ref, wk_ref, wv_ref, cos_ref, sin_ref,
                qq_ref, k_ref, vt_ref):
    t = x_ref.shape[0]
    h = (x_ref[...] * (1.0 + sc_ref[...]) + sh_ref[...]).astype(BF16)
    q = jnp.dot(h, wq_ref[...], preferred_element_type=F32)
    k = jnp.dot(h, wk_ref[...], preferred_element_type=F32)
    v = jnp.dot(h, wv_ref[...], preferred_element_type=F32)
    cos = cos_ref[...]
    sin = sin_ref[...]
    lane = lax.broadcasted_iota(jnp.int32, (t, LANES), 1)
    first_half = (lane % DIFF_QK_DIM) < (DIFF_QK_DIM // 2)
    comp = lax.broadcasted_iota(jnp.int32, (LANES, t), 0) < DIFF_QK_DIM
    q_scale = DIFF_QK_DIM ** -0.5 * LOG2_E
    ones = jnp.ones((V_AUG_ROWS - DIFF_V_DIM, t), BF16)
    row_chunk = lax.broadcasted_iota(jnp.int32, (t, LANES), 0) // CHUNK
    chunk_onehot = jnp.where(row_chunk == lane, 1.0, 0.0).astype(BF16)
    for hd in range(DIFF_HEADS):
        cols = slice(hd * LANES, (hd + 1) * LANES)
        k_ref[hd, :, :LANES] = _rope(k[:, cols], cos, sin, first_half).astype(BF16)
        k_ref[hd, :, LANES:] = chunk_onehot
        qt = (_rope(q[:, cols], cos, sin, first_half) * q_scale).T
        qq_ref[hd, 0, :, :t] = jnp.where(comp, qt, 0.0).astype(BF16)
        qq_ref[hd, 0, :, t:] = jnp.where(comp, 0.0, qt).astype(BF16)
        vt_ref[hd, 0, :DIFF_V_DIM, :] = v[:, cols].T.astype(BF16)
        vt_ref[hd, 0, DIFF_V_DIM:, :] = ones


def _qkv_rope(x, shift, scale, w_q, w_k, w_v, cos, sin_signed):
    bsz, s, d = x.shape
    t = ATT_T
    nt = s // t
    width = w_q.shape[1]
    vec = pl.BlockSpec((None, 1, d), lambda b, i: (b, 0, 0))
    wspec = pl.BlockSpec((d, width), lambda b, i: (0, 0))
    tab = pl.BlockSpec((t, LANES), lambda b, i: (i, 0))
    return pl.pallas_call(
        _qkv_kernel,
        grid=(bsz, nt),
        in_specs=[pl.BlockSpec((None, t, d), lambda b, i: (b, i, 0)), vec, vec,
                  wspec, wspec, wspec, tab, tab],
        out_specs=[
            pl.BlockSpec((None, DIFF_HEADS, 1, LANES, 2 * t), lambda b, i: (b, 0, i, 0, 0)),
            pl.BlockSpec((None, DIFF_HEADS, t, 2 * LANES), lambda b, i: (b, 0, i, 0)),
            pl.BlockSpec((None, DIFF_HEADS, 1, V_AUG_ROWS, t), lambda b, i: (b, 0, i, 0, 0)),
        ],
        out_shape=[
            jax.ShapeDtypeStruct((bsz, DIFF_HEADS, nt, LANES, 2 * t), BF16),
            jax.ShapeDtypeStruct((bsz, DIFF_HEADS, s, 2 * LANES), BF16),
            jax.ShapeDtypeStruct((bsz, DIFF_HEADS, nt, V_AUG_ROWS, t), BF16),
        ],
        compiler_params=_params("arbitrary", "arbitrary"),
        name="qkv_rope",
    )(x, shift, scale, w_q, w_k, w_v, cos, sin_signed)


def _attn_kernel(lam_init, n_steps, sched_ref, qq_ref, k_ref, vt_ref, bias_ref,
                 lq1_ref, lk1_ref, lq2_ref, lk2_ref, sg_ref, o_ref,
                 m_ref, acc_ref, fin_ref, s0_ref, s1_ref):
    t = ATT_T
    m_ref[...] = jnp.full_like(m_ref, NEG_INF)
    acc_ref[...] = jnp.zeros_like(acc_ref)

    def produce(n, s_ref):
        qi, kj, dg = sched_ref[0, n], sched_ref[1, n], sched_ref[2, n]
        rhs = jnp.concatenate([qq_ref[qi], bias_ref[dg]], axis=0)
        kblk = k_ref[pl.ds(pl.multiple_of(kj * t, t), t), :]
        s_ref[...] = jnp.dot(kblk, rhs, preferred_element_type=F32)

    def consume(n, s_ref):
        qi, kj = sched_ref[0, n], sched_ref[1, n]
        s = s_ref[...]
        m_prev = jnp.where(kj == 0, NEG_INF, m_ref[...])
        m_new = jnp.maximum(m_prev, jnp.max(s, axis=0, keepdims=True))
        alpha = jnp.exp2(m_prev - m_new)
        p = jnp.exp2(s - m_new).astype(BF16)
        pv = jnp.dot(vt_ref[kj], p, preferred_element_type=F32)
        acc = alpha * acc_ref[...] + pv
        acc_ref[...] = acc
        fin_ref[qi] = acc
        m_ref[...] = m_new

    produce(0, s0_ref)
    produce(1, s1_ref)

    def trip(g, carry):
        for u in range(ATT_UNROLL):
            n = g * ATT_UNROLL + u
            cur = s0_ref if u % 2 == 0 else s1_ref
            consume(n, cur)
            produce(n + 2, cur)
        return carry

    lax.fori_loop(0, n_steps // ATT_UNROLL, trip, 0)

    lam = (jnp.exp(jnp.sum(lq1_ref[...] * lk1_ref[...], keepdims=True))
           - jnp.exp(jnp.sum(lq2_ref[...] * lk2_ref[...], keepdims=True))
           + lam_init)
    sub_g = sg_ref[...] * (1.0 - lam_init)

    def finalize(qi, carry):
        acc = fin_ref[qi]
        o = acc[:DIFF_V_DIM, :] / acc[DIFF_V_DIM:DIFF_V_DIM + 1, :]
        o = o[:, :t] - lam * o[:, t:]
        ms = jnp.mean(o * o, axis=0, keepdims=True)
        o = o * lax.rsqrt(ms + LN_EPS) * sub_g
        o_ref[pl.ds(pl.multiple_of(qi * t, t), t), :] = o.T.astype(BF16)
        return carry

    lax.fori_loop(0, fin_ref.shape[0], finalize, 0)


def _attn_schedule(nt):
    steps = [(qi, kj, int(kj == qi)) for qi in range(nt) for kj in range(qi + 1)]
    n_steps = len(steps)
    steps += [(0, 0, 1)] * 2
    return jnp.asarray(steps, jnp.int32).T, n_steps


def _attn_mask_bias(t):
    key_chunk = jnp.arange(LANES)[:, None]
    q_chunk = (jnp.arange(2 * t)[None, :] % t) // CHUNK
    hidden = (key_chunk < t // CHUNK) & (key_chunk > q_chunk)
    diag = jnp.where(hidden, NEG_INF, 0.0)
    return jnp.stack([jnp.zeros_like(diag), diag]).astype(BF16)


def _diff_attn(qq, k, vt, lq1, lk1, lq2, lk2, sub_g, lam_init):
    bsz, heads, nt, _, _ = qq.shape
    t = ATT_T
    s = nt * t
    sched, n_steps = _attn_schedule(nt)
    assert n_steps % ATT_UNROLL == 0 and ATT_UNROLL % 2 == 0
    lam_spec = pl.BlockSpec((1, DIFF_QK_DIM), lambda b, h, sc: (0, 0))
    grid_spec = pltpu.PrefetchScalarGridSpec(
        num_scalar_prefetch=1,
        grid=(bsz, heads),
        in_specs=[
            pl.BlockSpec((None, None, nt, LANES, 2 * t), lambda b, h, sc: (b, h, 0, 0, 0)),
            pl.BlockSpec((None, None, s, 2 * LANES), lambda b, h, sc: (b, h, 0, 0)),
            pl.BlockSpec((None, None, nt, V_AUG_ROWS, t), lambda b, h, sc: (b, h, 0, 0, 0)),
            pl.BlockSpec((2, LANES, 2 * t), lambda b, h, sc: (0, 0, 0)),
            lam_spec, lam_spec, lam_spec, lam_spec,
            pl.BlockSpec((DIFF_V_DIM, 1), lambda b, h, sc: (0, 0)),
        ],
        out_specs=pl.BlockSpec((None, s, LANES), lambda b, h, sc: (b, 0, h)),
        scratch_shapes=[pltpu.VMEM((1, 2 * t), F32),
                        pltpu.VMEM((V_AUG_ROWS, 2 * t), F32),
                        pltpu.VMEM((nt, V_AUG_ROWS, 2 * t), F32),
                        pltpu.VMEM((t, 2 * t), F32), pltpu.VMEM((t, 2 * t), F32)],
    )
    return pl.pallas_call(
        functools.partial(_attn_kernel, lam_init, n_steps),
        grid_spec=grid_spec,
        out_shape=jax.ShapeDtypeStruct((bsz, s, heads * DIFF_V_DIM), BF16),
        compiler_params=_params("arbitrary", "arbitrary"),
        name="diff_attn",
    )(sched, qq, k, vt, _attn_mask_bias(t), lq1.reshape(1, -1), lk1.reshape(1, -1),
      lq2.reshape(1, -1), lk2.reshape(1, -1), sub_g.reshape(-1, 1))


def _out_kernel(alpha, x_ref, gt_ref, a_ref, b_ref, wa_ref, wb_ref, lg_ref, lb_ref,
                o_ref):
    gate = 1.0 + gt_ref[...]
    for c in range(x_ref.shape[0] // ROW_CHUNK):
        rows = pl.ds(c * ROW_CHUNK, ROW_CHUNK)
        y = jnp.dot(a_ref[rows, :], wa_ref[...], preferred_element_type=F32)
        y = y + jnp.dot(b_ref[rows, :], wb_ref[...], preferred_element_type=F32)
        z = alpha * x_ref[rows, :] + gate * y
        o_ref[rows, :] = _layer_norm_rows(z, lg_ref[...], lb_ref[...])


def _out_ln(x, gate, out_a, out_b, w_out, ln_g, ln_b, alpha):
    bsz, s, d = x.shape
    half = out_a.shape[-1]
    tile = pl.BlockSpec((None, MIX_TM, d), lambda b, i: (b, i, 0))
    act = pl.BlockSpec((None, MIX_TM, half), lambda b, i: (b, i, 0))
    row = pl.BlockSpec((1, d), lambda b, i: (0, 0))
    return pl.pallas_call(
        functools.partial(_out_kernel, alpha),
        grid=(bsz, s // MIX_TM),
        in_specs=[tile, pl.BlockSpec((None, 1, d), lambda b, i: (b, 0, 0)), act, act,
                  pl.BlockSpec((half, d), lambda b, i: (0, 0)),
                  pl.BlockSpec((half, d), lambda b, i: (1, 0)),
                  row, row],
        out_specs=tile,
        out_shape=jax.ShapeDtypeStruct((bsz, s, d), F32),
        compiler_params=_params("arbitrary", "arbitrary"),
        name="out_ln",
    )(x, gate, out_a, out_b, w_out, w_out, ln_g.reshape(1, d), ln_b.reshape(1, d))


def _rope_tables(s):
    half = DIFF_QK_DIM // 2
    pos = jnp.arange(s, dtype=F32)
    inv_freq = ROPE_THETA ** (-jnp.arange(0, DIFF_QK_DIM, 2, dtype=F32) / DIFF_QK_DIM)
    ang = pos[:, None] * inv_freq[None, :]
    cos, sin = jnp.cos(ang), jnp.sin(ang)
    cos = jnp.concatenate([cos, cos, cos, cos], axis=-1)
    sin_signed = jnp.concatenate([-sin, sin, -sin, sin], axis=-1)
    assert cos.shape[-1] == 4 * half == LANES
    return cos, sin_signed


def kernel(x, c, w_ada, b_ada, ffn1_w_gu, ffn1_w_down, ln1_g, ln1_b, w_in, gmlp_ln_g, gmlp_ln_b, gmlp_w_s, gmlp_b_s, lambda_q1, lambda_k1, lambda_q2, lambda_k2, diff_subln_g, w_out, ln2_g, ln2_b, ffn2_w_gu, ffn2_w_down, ln3_g, ln3_b):
    bsz, s, d = x.shape
    depth = w_ada.shape[0]
    alpha = (2 * depth) ** 0.25
    gw = gmlp_ln_g.shape[-1]
    qk = DIFF_HEADS * 2 * DIFF_QK_DIM
    c1, c2, c3, c4 = gw, 2 * gw, 2 * gw + qk, 2 * gw + 2 * qk
    cos, sin_signed = _rope_tables(s)
    c_pad = jnp.zeros((8, d), F32).at[:bsz].set(c)

    for l in range(depth):
        mod = _adaln(c_pad, w_ada[l], b_ada[l])[:bsz]
        mod = mod.reshape(bsz, N_MOD, 1, d)
        s1, sc1, g1, s2, sc2, g2, s3, sc3, g3 = [mod[:, n] for n in range(N_MOD)]

        x = _ffn_ln(x, s1, sc1, g1, ffn1_w_gu[l].astype(BF16),
                    ffn1_w_down[l].astype(BF16), ln1_g[l], ln1_b[l], alpha)

        w_in_b = w_in[l].astype(BF16)
        out_a, w_out_b, w_gu2_b, w_down2_b = _gmlp(
            x, s2, sc2, w_in_b[:, :c1], w_in_b[:, c1:c2], gmlp_ln_g[l], gmlp_ln_b[l],
            gmlp_w_s[l], jnp.transpose(gmlp_b_s[l]),
            cast_weights=(w_out[l], ffn2_w_gu[l], ffn2_w_down[l]))
        qq, k, vt = _qkv_rope(x, s2, sc2, w_in_b[:, c2:c3], w_in_b[:, c3:c4],
                              w_in_b[:, c4:], cos, sin_signed)
        lam_init = 0.8 - 0.6 * math.exp(-0.3 * l)
        out_b = _diff_attn(qq, k, vt, lambda_q1[l], lambda_k1[l], lambda_q2[l],
                           lambda_k2[l], diff_subln_g[l], lam_init)
        x = _out_ln(x, g2, out_a, out_b, w_out_b, ln2_g[l], ln2_b[l], alpha)

        x = _ffn_ln(x, s3, sc3, g3, w_gu2_b, w_down2_b, ln3_g[l], ln3_b[l], alpha)
    return x
```

```python
import functools
import math

import jax
import jax.numpy as jnp
import numpy as np
from jax import lax
from jax.experimental import pallas as pl
from jax.experimental.pallas import tpu as pltpu

F32 = jnp.float32
BF16 = jnp.bfloat16

CHUNK = 64
GMLP_BLOCK = 128
GMLP_GROUPS = 8
DIFF_HEADS = 8
DIFF_V_DIM = 128
DIFF_QK_DIM = 64
N_MOD = 9
ROPE_THETA = 10000.0
LN_EPS = 1e-5
NEG_INF = -1e30
LOG2_E = 1.4426950408889634
V_AUG_ROWS = DIFF_V_DIM + 16

LANES = 128
VMEM_LIMIT_BYTES = 56 * 1024 * 1024
FFN_VMEM_LIMIT_BYTES = 61 * 1024 * 1024

ADA_TN = 1024
FFN_TM = 1024
FFN_TF = 512
FFN_UP_COLS = 256
FFN_DOWN_COLS = 256
MIX_TM = 512
ROW_CHUNK = 256
PROJ_COLS = 256
ATT_T = 512
ATT_UNROLL = 34


def _params(*sem, vmem_limit_bytes=VMEM_LIMIT_BYTES):
    return pltpu.CompilerParams(dimension_semantics=sem,
                                vmem_limit_bytes=vmem_limit_bytes)


def _layer_norm_rows(z, g, b):
    mu = jnp.mean(z, axis=-1, keepdims=True)
    zc = z - mu
    var = jnp.mean(zc * zc, axis=-1, keepdims=True)
    return zc * lax.rsqrt(var + LN_EPS) * g + b


def _adaln_kernel(c_ref, w_ref, b_ref, o_ref):
    c = c_ref[...]
    ca = (c / (1.0 + jnp.exp(-c))).astype(BF16)
    o_ref[...] = jnp.dot(ca, w_ref[...].astype(BF16),
                         preferred_element_type=F32) + b_ref[...]


def _adaln(c_pad, w, b):
    d, n = w.shape
    return pl.pallas_call(
        _adaln_kernel,
        grid=(n // ADA_TN,),
        in_specs=[pl.BlockSpec((8, d), lambda j: (0, 0)),
                  pl.BlockSpec((d, ADA_TN), lambda j: (0, j)),
                  pl.BlockSpec((1, ADA_TN), lambda j: (0, j))],
        out_specs=pl.BlockSpec((8, ADA_TN), lambda j: (0, j)),
        out_shape=jax.ShapeDtypeStruct((8, n), F32),
        compiler_params=_params("arbitrary"),
        name="adaln",
    )(c_pad, w, b.reshape(1, n))


def _ffn_kernel(alpha, x_ref, sh_ref, sc_ref, gt_ref, wg_ref, wu_ref, wd_ref,
                lg_ref, lb_ref, o_ref, h_ref, a_ref):
    j = pl.program_id(2)
    last = pl.num_programs(2) - 1

    def hidden():
        h = h_ref[...]
        for c in range(a_ref.shape[1] // FFN_UP_COLS):
            cols = slice(c * FFN_UP_COLS, (c + 1) * FFN_UP_COLS)
            g = jnp.dot(h, wg_ref[:, cols], preferred_element_type=F32)
            u = jnp.dot(h, wu_ref[:, cols], preferred_element_type=F32)
            a_ref[:, cols] = (g / (1.0 + jnp.exp(-g)) * u).astype(BF16)

    def down(first):
        a = a_ref[...]
        for c in range(o_ref.shape[1] // FFN_DOWN_COLS):
            cols = slice(c * FFN_DOWN_COLS, (c + 1) * FFN_DOWN_COLS)
            y = jnp.dot(a, wd_ref[:, cols], preferred_element_type=F32)
            o_ref[:, cols] = y if first else o_ref[:, cols] + y

    @pl.when(j == 0)
    def _():
        h_ref[...] = (x_ref[...] * (1.0 + sc_ref[...]) + sh_ref[...]).astype(BF16)
        hidden()
        down(first=True)

    @pl.when(jnp.logical_and(j > 0, j < last))
    def _():
        hidden()
        down(first=False)

    @pl.when(j == last)
    def _():
        hidden()
        gate = 0.5 * (1.0 + gt_ref[...])
        for r in range(o_ref.shape[0] // ROW_CHUNK):
            rows = pl.ds(r * ROW_CHUNK, ROW_CHUNK)
            y = o_ref[rows, :] + jnp.dot(a_ref[rows, :], wd_ref[...],
                                         preferred_element_type=F32)
            z = alpha * x_ref[rows, :] + gate * y
            o_ref[rows, :] = _layer_norm_rows(z, lg_ref[...], lb_ref[...])


def _ffn_ln(x, shift, scale, gate, w_gu, w_down, ln_g, ln_b, alpha):
    bsz, s, d = x.shape
    f = w_down.shape[0]
    nf = f // FFN_TF
    vec = pl.BlockSpec((None, 1, d), lambda b, i, j: (b, 0, 0))
    row = pl.BlockSpec((1, d), lambda b, i, j: (0, 0))
    tile = pl.BlockSpec((None, FFN_TM, d), lambda b, i, j: (b, i, 0))
    return pl.pallas_call(
        functools.partial(_ffn_kernel, alpha),
        grid=(bsz, s // FFN_TM, nf),
        in_specs=[tile, vec, vec, vec,
                  pl.BlockSpec((d, FFN_TF), lambda b, i, j: (0, j)),
                  pl.BlockSpec((d, FFN_TF), lambda b, i, j: (0, nf + j)),
                  pl.BlockSpec((FFN_TF, d), lambda b, i, j: (j, 0)),
                  row, row],
        out_specs=tile,
        out_shape=jax.ShapeDtypeStruct((bsz, s, d), F32),
        scratch_shapes=[pltpu.VMEM((FFN_TM, d), BF16), pltpu.VMEM((FFN_TM, FFN_TF), BF16)],
        compiler_params=_params("arbitrary", "arbitrary", "arbitrary",
                                vmem_limit_bytes=FFN_VMEM_LIMIT_BYTES),
        name="ffn_ln",
    )(x, shift, scale, gate, w_gu, w_gu, w_down, ln_g.reshape(1, d), ln_b.reshape(1, d))


def _gmlp_kernel(n_cast, x_ref, sh_ref, sc_ref, wu_ref, wv_ref, lg_ref, lb_ref, ws_ref,
                 bst_ref, *rest):
    cast_in, o_ref, cast_out = rest[:n_cast], rest[n_cast], rest[n_cast + 1:]
    for src, dst in zip(cast_in, cast_out):
        dst[...] = src[...].astype(BF16)

    pi = lax.broadcasted_iota(jnp.int32, (GMLP_BLOCK, GMLP_BLOCK), 0) // CHUNK
    pj = lax.broadcasted_iota(jnp.int32, (GMLP_BLOCK, GMLP_BLOCK), 1) // CHUNK
    keep = pj <= pi
    bst = bst_ref[...]

    h = (x_ref[...] * (1.0 + sc_ref[...]) + sh_ref[...]).astype(BF16)

    def gelu_proj(w_ref):
        parts = []
        for c in range(w_ref.shape[1] // PROJ_COLS):
            cols = slice(c * PROJ_COLS, (c + 1) * PROJ_COLS)
            parts.append(jax.nn.gelu(jnp.dot(h, w_ref[:, cols], preferred_element_type=F32)))
        return jnp.concatenate(parts, axis=1)

    v = _layer_norm_rows(gelu_proj(wv_ref), lg_ref[...], lb_ref[...]).astype(BF16)
    u = gelu_proj(wu_ref)
    n_blk = x_ref.shape[0] // GMLP_BLOCK
    for g in range(GMLP_GROUPS):
        w = jnp.where(keep, ws_ref[g], 0.0).astype(BF16)
        bias = bst[:, g:g + 1]
        cols = slice(g * LANES, (g + 1) * LANES)
        v_blocks = jnp.concatenate(
            [v[blk * GMLP_BLOCK:(blk + 1) * GMLP_BLOCK, cols] for blk in range(n_blk)], axis=1)
        mixed = jnp.dot(w, v_blocks, preferred_element_type=F32) + bias
        for blk in range(n_blk):
            rows = slice(blk * GMLP_BLOCK, (blk + 1) * GMLP_BLOCK)
            o_ref[rows, cols] = (u[rows, cols]
                                 * mixed[:, blk * LANES:(blk + 1) * LANES]).astype(BF16)


def _gmlp(x, shift, scale, w_u, w_v, ln_g, ln_b, w_s, b_s_t, cast_weights):
    bsz, s, d = x.shape
    width = w_u.shape[1]
    nt = s // MIX_TM
    vec = pl.BlockSpec((None, 1, d), lambda b, i: (b, 0, 0))
    full = lambda shape: pl.BlockSpec(shape, lambda b, i: (0,) * len(shape))
    slab_specs = []
    for w in cast_weights:
        rows = w.shape[0] // (bsz * nt)
        assert rows * bsz * nt == w.shape[0] and rows % 16 == 0, w.shape
        slab_specs.append(pl.BlockSpec((rows, w.shape[1]), lambda b, i: (b * nt + i, 0)))
    return pl.pallas_call(
        functools.partial(_gmlp_kernel, len(cast_weights)),
        grid=(bsz, nt),
        in_specs=[pl.BlockSpec((None, MIX_TM, d), lambda b, i: (b, i, 0)), vec, vec,
                  full((d, width)), full((d, width)),
                  full((1, width)), full((1, width)),
                  full(w_s.shape), full(b_s_t.shape)] + slab_specs,
        out_specs=[pl.BlockSpec((None, MIX_TM, width), lambda b, i: (b, i, 0))] + slab_specs,
        out_shape=[jax.ShapeDtypeStruct((bsz, s, width), BF16)]
        + [jax.ShapeDtypeStruct(w.shape, BF16) for w in cast_weights],
        compiler_params=_params("arbitrary", "arbitrary"),
        name="gmlp",
    )(x, shift, scale, w_u, w_v, ln_g.reshape(1, width), ln_b.reshape(1, width),
      w_s, b_s_t, *cast_weights)


def _rope(x, cos, sin_signed, first_half):
    rot = jnp.where(first_half, pltpu.roll(x, 3 * LANES // 4, 1),
                    pltpu.roll(x, LANES // 4, 1))
    return x * cos + rot * sin_signed


def _qkv_kernel(x_ref, sh_ref, sc_ref, wq_ref, wk_ref, wv_ref, cos_ref, sin_ref,
                qq_ref, k_ref, vt_ref):
    t = x_ref.shape[0]
    h = (x_ref[...] * (1.0 + sc_ref[...]) + sh_ref[...]).astype(BF16)
    cos = cos_ref[...]
    sin = sin_ref[...]
    lane = lax.broadcasted_iota(jnp.int32, (t, LANES), 1)
    first_half = (lane % DIFF_QK_DIM) < (DIFF_QK_DIM // 2)
    comp = lax.broadcasted_iota(jnp.int32, (LANES, t), 0) < DIFF_QK_DIM
    q_scale = DIFF_QK_DIM ** -0.5 * LOG2_E
    ones = jnp.ones((V_AUG_ROWS - DIFF_V_DIM, t), BF16)
    row_chunk = lax.broadcasted_iota(jnp.int32, (t, LANES), 0) // CHUNK
    chunk_onehot = jnp.where(row_chunk == lane, 1.0, 0.0).astype(BF16)
    q = jnp.dot(h, wq_ref[...], preferred_element_type=F32)
    k = jnp.dot(h, wk_ref[...], preferred_element_type=F32)
    v = jnp.dot(h, wv_ref[...], preferred_element_type=F32)
    for hd in range(DIFF_HEADS):
        cols = slice(hd * LANES, (hd + 1) * LANES)
        k_ref[hd, :, :LANES] = _rope(k[:, cols], cos, sin, first_half).astype(BF16)
        k_ref[hd, :, LANES:] = chunk_onehot
        qt = (_rope(q[:, cols], cos, sin, first_half) * q_scale).T
        qq_ref[hd, 0, :, :t] = jnp.where(comp, qt, 0.0).astype(BF16)
        qq_ref[hd, 0, :, t:] = jnp.where(comp, 0.0, qt).astype(BF16)
        vt_ref[hd, 0, :DIFF_V_DIM, :] = v[:, cols].T.astype(BF16)
        vt_ref[hd, 0, DIFF_V_DIM:, :] = ones


def _qkv_rope(x, shift, scale, w_q, w_k, w_v, cos, sin_signed):
    bsz, s, d = x.shape
    t = ATT_T
    nt = s // t
    width = w_q.shape[1]
    vec = pl.BlockSpec((None, 1, d), lambda b, i: (b, 0, 0))
    wspec = pl.BlockSpec((d, width), lambda b, i: (0, 0))
    tab = pl.BlockSpec((t, LANES), lambda b, i: (i, 0))
    return pl.pallas_call(
        _qkv_kernel,
        grid=(bsz, nt),
        in_specs=[pl.BlockSpec((None, t, d), lambda b, i: (b, i, 0)), vec, vec,
                  wspec, wspec, wspec, tab, tab],
        out_specs=[
            pl.BlockSpec((None, DIFF_HEADS, 1, LANES, 2 * t), lambda b, i: (b, 0, i, 0, 0)),
            pl.BlockSpec((None, DIFF_HEADS, t, 2 * LANES), lambda b, i: (b, 0, i, 0)),
            pl.BlockSpec((None, DIFF_HEADS, 1, V_AUG_ROWS, t), lambda b, i: (b, 0, i, 0, 0)),
        ],
        out_shape=[
            jax.ShapeDtypeStruct((bsz, DIFF_HEADS, nt, LANES, 2 * t), BF16),
            jax.ShapeDtypeStruct((bsz, DIFF_HEADS, s, 2 * LANES), BF16),
            jax.ShapeDtypeStruct((bsz, DIFF_HEADS, nt, V_AUG_ROWS, t), BF16),
        ],
        compiler_params=_params("arbitrary", "arbitrary"),
        name="qkv_rope",
    )(x, shift, scale, w_q, w_k, w_v, cos, sin_signed)


def _attn_kernel(lam_init, n_steps, sched_ref, qq_ref, k_ref, vt_ref, bias_ref,
                 lq1_ref, lk1_ref, lq2_ref, lk2_ref, sg_ref, o_ref,
                 m_ref, acc_ref, fin_ref, s0_ref, s1_ref):
    t = ATT_T
    m_ref[...] = jnp.full_like(m_ref, NEG_INF)
    acc_ref[...] = jnp.zeros_like(acc_ref)

    def produce(n, s_ref):
        qi, kj, dg = sched_ref[0, n], sched_ref[1, n], sched_ref[2, n]
        rhs = jnp.concatenate([qq_ref[qi], bias_ref[dg]], axis=0)
        kblk = k_ref[pl.ds(pl.multiple_of(kj * t, t), t), :]
        s_ref[...] = jnp.dot(kblk, rhs, preferred_element_type=F32)

    def consume(n, s_ref):
        qi, kj = sched_ref[0, n], sched_ref[1, n]
        s = s_ref[...]
        m_prev = jnp.where(kj == 0, NEG_INF, m_ref[...])
        m_new = jnp.maximum(m_prev, jnp.max(s, axis=0, keepdims=True))
        alpha = jnp.exp2(m_prev - m_new)
        p = jnp.exp2(s - m_new).astype(BF16)
        pv = jnp.dot(vt_ref[kj], p, preferred_element_type=F32)
        acc = alpha * acc_ref[...] + pv
        acc_ref[...] = acc
        fin_ref[qi] = acc
        m_ref[...] = m_new

    produce(0, s0_ref)
    produce(1, s1_ref)

    def trip(g, carry):
        for u in range(ATT_UNROLL):
            n = g * ATT_UNROLL + u
            cur = s0_ref if u % 2 == 0 else s1_ref
            consume(n, cur)
            produce(n + 2, cur)
        return carry

    lax.fori_loop(0, n_steps // ATT_UNROLL, trip, 0)

    lam = (jnp.exp(jnp.sum(lq1_ref[...] * lk1_ref[...], keepdims=True))
           - jnp.exp(jnp.sum(lq2_ref[...] * lk2_ref[...], keepdims=True))
           + lam_init)
    sub_g = sg_ref[...] * (1.0 - lam_init)

    def finalize(qi, carry):
        acc = fin_ref[qi]
        o = acc[:DIFF_V_DIM, :] / acc[DIFF_V_DIM:DIFF_V_DIM + 1, :]
        o = o[:, :t] - lam * o[:, t:]
        ms = jnp.mean(o * o, axis=0, keepdims=True)
        o = o * lax.rsqrt(ms + LN_EPS) * sub_g
        o_ref[pl.ds(pl.multiple_of(qi * t, t), t), :] = o.T.astype(BF16)
        return carry

    lax.fori_loop(0, fin_ref.shape[0], finalize, 0)


def _attn_schedule(nt):
    steps = [(qi, kj, int(kj == qi)) for qi in range(nt) for kj in range(qi + 1)]
    n_steps = len(steps)
    steps += [(0, 0, 1)] * 2
    return jnp.asarray(steps, jnp.int32).T, n_steps


def _attn_mask_bias(t):
    key_chunk = np.arange(LANES)[:, None]
    q_chunk = (np.arange(2 * t)[None, :] % t) // CHUNK
    hidden = (key_chunk < t // CHUNK) & (key_chunk > q_chunk)
    diag = np.where(hidden, NEG_INF, 0.0).astype(np.float32)
    return jnp.asarray(np.stack([np.zeros_like(diag), diag]), dtype=BF16)


def _diff_attn(qq, k, vt, lq1, lk1, lq2, lk2, sub_g, lam_init):
    bsz, heads, nt, _, _ = qq.shape
    t = ATT_T
    s = nt * t
    sched, n_steps = _attn_schedule(nt)
    assert n_steps % ATT_UNROLL == 0 and ATT_UNROLL % 2 == 0
    lam_spec = pl.BlockSpec((1, DIFF_QK_DIM), lambda b, h, sc: (0, 0))
    grid_spec = pltpu.PrefetchScalarGridSpec(
        num_scalar_prefetch=1,
        grid=(bsz, heads),
        in_specs=[
            pl.BlockSpec((None, None, nt, LANES, 2 * t), lambda b, h, sc: (b, h, 0, 0, 0)),
            pl.BlockSpec((None, None, s, 2 * LANES), lambda b, h, sc: (b, h, 0, 0)),
            pl.BlockSpec((None, None, nt, V_AUG_ROWS, t), lambda b, h, sc: (b, h, 0, 0, 0)),
            pl.BlockSpec((2, LANES, 2 * t), lambda b, h, sc: (0, 0, 0)),
            lam_spec, lam_spec, lam_spec, lam_spec,
            pl.BlockSpec((DIFF_V_DIM, 1), lambda b, h, sc: (0, 0)),
        ],
        out_specs=pl.BlockSpec((None, s, LANES), lambda b, h, sc: (b, 0, h)),
        scratch_shapes=[pltpu.VMEM((1, 2 * t), F32),
                        pltpu.VMEM((V_AUG_ROWS, 2 * t), F32),
                        pltpu.VMEM((nt, V_AUG_ROWS, 2 * t), F32),
                        pltpu.VMEM((t, 2 * t), F32), pltpu.VMEM((t, 2 * t), F32)],
    )
    return pl.pallas_call(
        functools.partial(_attn_kernel, lam_init, n_steps),
        grid_spec=grid_spec,
        out_shape=jax.ShapeDtypeStruct((bsz, s, heads * DIFF_V_DIM), BF16),
        compiler_params=_params("arbitrary", "arbitrary"),
        name="diff_attn",
    )(sched, qq, k, vt, _attn_mask_bias(t), lq1.reshape(1, -1), lk1.reshape(1, -1),
      lq2.reshape(1, -1), lk2.reshape(1, -1), sub_g.reshape(-1, 1))


def _out_kernel(alpha, x_ref, gt_ref, a_ref, b_ref, wa_ref, wb_ref, lg_ref, lb_ref,
                o_ref):
    gate = 1.0 + gt_ref[...]
    for c in range(x_ref.shape[0] // ROW_CHUNK):
        rows = pl.ds(c * ROW_CHUNK, ROW_CHUNK)
        y = jnp.dot(a_ref[rows, :], wa_ref[...], preferred_element_type=F32)
        y = y + jnp.dot(b_ref[rows, :], wb_ref[...], preferred_element_type=F32)
        z = alpha * x_ref[rows, :] + gate * y
        o_ref[rows, :] = _layer_norm_rows(z, lg_ref[...], lb_ref[...])


def _out_ln(x, gate, out_a, out_b, w_out, ln_g, ln_b, alpha):
    bsz, s, d = x.shape
    half = out_a.shape[-1]
    tile = pl.BlockSpec((None, MIX_TM, d), lambda b, i: (b, i, 0))
    act = pl.BlockSpec((None, MIX_TM, half), lambda b, i: (b, i, 0))
    row = pl.BlockSpec((1, d), lambda b, i: (0, 0))
    return pl.pallas_call(
        functools.partial(_out_kernel, alpha),
        grid=(bsz, s // MIX_TM),
        in_specs=[tile, pl.BlockSpec((None, 1, d), lambda b, i: (b, 0, 0)), act, act,
                  pl.BlockSpec((half, d), lambda b, i: (0, 0)),
                  pl.BlockSpec((half, d), lambda b, i: (1, 0)),
                  row, row],
        out_specs=tile,
        out_shape=jax.ShapeDtypeStruct((bsz, s, d), F32),
        compiler_params=_params("arbitrary", "arbitrary"),
        name="out_ln",
    )(x, gate, out_a, out_b, w_out, w_out, ln_g.reshape(1, d), ln_b.reshape(1, d))


def _rope_tables(s):
    half = DIFF_QK_DIM // 2
    pos = np.arange(s, dtype=np.float64)
    inv_freq = ROPE_THETA ** (-np.arange(0, DIFF_QK_DIM, 2, dtype=np.float64) / DIFF_QK_DIM)
    ang = pos[:, None] * inv_freq[None, :]
    cos = jnp.asarray(np.cos(ang), dtype=F32)
    sin = jnp.asarray(np.sin(ang), dtype=F32)
    cos = jnp.concatenate([cos, cos, cos, cos], axis=-1)
    sin_signed = jnp.concatenate([-sin, sin, -sin, sin], axis=-1)
    assert cos.shape[-1] == 4 * half == LANES
    return cos, sin_signed


def kernel(x, c, w_ada, b_ada, ffn1_w_gu, ffn1_w_down, ln1_g, ln1_b, w_in, gmlp_ln_g, gmlp_ln_b, gmlp_w_s, gmlp_b_s, lambda_q1, lambda_k1, lambda_q2, lambda_k2, diff_subln_g, w_out, ln2_g, ln2_b, ffn2_w_gu, ffn2_w_down, ln3_g, ln3_b):
    bsz, s, d = x.shape
    depth = w_ada.shape[0]
    alpha = (2 * depth) ** 0.25
    gw = gmlp_ln_g.shape[-1]
    qk = DIFF_HEADS * 2 * DIFF_QK_DIM
    c1, c2, c3, c4 = gw, 2 * gw, 2 * gw + qk, 2 * gw + 2 * qk
    cos, sin_signed = _rope_tables(s)
    c_pad = jnp.zeros((8, d), F32).at[:bsz].set(c)

    for l in range(depth):
        mod = _adaln(c_pad, w_ada[l], b_ada[l])[:bsz]
        mod = mod.reshape(bsz, N_MOD, 1, d)
        s1, sc1, g1, s2, sc2, g2, s3, sc3, g3 = [mod[:, n] for n in range(N_MOD)]

        x = _ffn_ln(x, s1, sc1, g1, ffn1_w_gu[l].astype(BF16),
                    ffn1_w_down[l].astype(BF16), ln1_g[l], ln1_b[l], alpha)

        w_in_b = w_in[l].astype(BF16)
        out_a, w_out_b, w_gu2_b, w_down2_b = _gmlp(
            x, s2, sc2, w_in_b[:, :c1], w_in_b[:, c1:c2], gmlp_ln_g[l], gmlp_ln_b[l],
            gmlp_w_s[l], jnp.transpose(gmlp_b_s[l]),
            cast_weights=(w_out[l], ffn2_w_gu[l], ffn2_w_down[l]))
        qq, k, vt = _qkv_rope(x, s2, sc2, w_in_b[:, c2:c3], w_in_b[:, c3:c4],
                              w_in_b[:, c4:], cos, sin_signed)
        lam_init = 0.8 - 0.6 * math.exp(-0.3 * l)
        out_b = _diff_attn(qq, k, vt, lambda_q1[l], lambda_k1[l], lambda_q2[l],
                           lambda_k2[l], diff_subln_g[l], lam_init)
        x = _out_ln(x, g2, out_a, out_b, w_out_b, ln2_g[l], ln2_b[l], alpha)

        x = _ffn_ln(x, s3, sc3, g3, w_gu2_b, w_down2_b, ln3_g[l], ln3_b[l], alpha)
    return x
```

```python
import functools
import math

import jax
import jax.numpy as jnp
import numpy as np
from jax import lax
from jax.experimental import pallas as pl
from jax.experimental.pallas import tpu as pltpu

F32 = jnp.float32
BF16 = jnp.bfloat16

CHUNK = 64
GMLP_BLOCK = 128
GMLP_GROUPS = 8
DIFF_HEADS = 8
DIFF_V_DIM = 128
DIFF_QK_DIM = 64
N_MOD = 9
ROPE_THETA = 10000.0
LN_EPS = 1e-5
NEG_INF = -1e30
LOG2_E = 1.4426950408889634
V_AUG_ROWS = DIFF_V_DIM + 16

LANES = 128
VMEM_LIMIT_BYTES = 56 * 1024 * 1024
FFN_VMEM_LIMIT_BYTES = 61 * 1024 * 1024

ADA_TN = 1024
FFN_TM = 1024
FFN_TF = 512
FFN_UP_COLS = 256
FFN_DOWN_COLS = 256
CAST_BLOCK = (128, 512)
MIX_TM = 512
ROW_CHUNK = 256
PROJ_COLS = 256
ATT_T = 512
ATT_UNROLL = 34


def _params(*sem, vmem_limit_bytes=VMEM_LIMIT_BYTES):
    return pltpu.CompilerParams(dimension_semantics=sem,
                                vmem_limit_bytes=vmem_limit_bytes)


def _layer_norm_rows(z, g, b):
    mu = jnp.mean(z, axis=-1, keepdims=True)
    zc = z - mu
    var = jnp.mean(zc * zc, axis=-1, keepdims=True)
    return zc * lax.rsqrt(var + LN_EPS) * g + b


def _adaln_kernel(c_ref, w_ref, b_ref, o_ref):
    c = c_ref[...]
    ca = (c / (1.0 + jnp.exp(-c))).astype(BF16)
    o_ref[...] = jnp.dot(ca, w_ref[...].astype(BF16),
                         preferred_element_type=F32) + b_ref[...]


def _adaln(c_pad, w, b):
    d, n = w.shape
    return pl.pallas_call(
        _adaln_kernel,
        grid=(n // ADA_TN,),
        in_specs=[pl.BlockSpec((8, d), lambda j: (0, 0)),
                  pl.BlockSpec((d, ADA_TN), lambda j: (0, j)),
                  pl.BlockSpec((1, ADA_TN), lambda j: (0, j))],
        out_specs=pl.BlockSpec((8, ADA_TN), lambda j: (0, j)),
        out_shape=jax.ShapeDtypeStruct((8, n), F32),
        compiler_params=_params("arbitrary"),
        name="adaln",
    )(c_pad, w, b.reshape(1, n))


def _ffn_kernel(alpha, has_cast, x_ref, sh_ref, sc_ref, gt_ref, wg_ref, wu_ref, wd_ref,
                lg_ref, lb_ref, *rest):
    if has_cast:
        cast_in, o_ref, cast_out, h_ref, a_ref = rest
        cast_out[...] = cast_in[...].astype(BF16)
    else:
        o_ref, h_ref, a_ref = rest
    j = pl.program_id(2)
    last = pl.num_programs(2) - 1

    def hidden():
        h = h_ref[...]
        for c in range(a_ref.shape[1] // FFN_UP_COLS):
            cols = slice(c * FFN_UP_COLS, (c + 1) * FFN_UP_COLS)
            g = jnp.dot(h, wg_ref[:, cols], preferred_element_type=F32)
            u = jnp.dot(h, wu_ref[:, cols], preferred_element_type=F32)
            a_ref[:, cols] = (g / (1.0 + jnp.exp(-g)) * u).astype(BF16)

    def down(first):
        a = a_ref[...]
        for c in range(o_ref.shape[1] // FFN_DOWN_COLS):
            cols = slice(c * FFN_DOWN_COLS, (c + 1) * FFN_DOWN_COLS)
            y = jnp.dot(a, wd_ref[:, cols], preferred_element_type=F32)
            o_ref[:, cols] = y if first else o_ref[:, cols] + y

    @pl.when(j == 0)
    def _():
        h_ref[...] = (x_ref[...] * (1.0 + sc_ref[...]) + sh_ref[...]).astype(BF16)
        hidden()
        down(first=True)

    @pl.when(jnp.logical_and(j > 0, j < last))
    def _():
        hidden()
        down(first=False)

    @pl.when(j == last)
    def _():
        hidden()
        gate = 0.5 * (1.0 + gt_ref[...])
        for r in range(o_ref.shape[0] // ROW_CHUNK):
            rows = pl.ds(r * ROW_CHUNK, ROW_CHUNK)
            y = o_ref[rows, :] + jnp.dot(a_ref[rows, :], wd_ref[...],
                                         preferred_element_type=F32)
            z = alpha * x_ref[rows, :] + gate * y
            o_ref[rows, :] = _layer_norm_rows(z, lg_ref[...], lb_ref[...])


def _ffn_ln(x, shift, scale, gate, w_gu, w_down, ln_g, ln_b, alpha, cast_weight=None):
    bsz, s, d = x.shape
    f = w_down.shape[0]
    nf = f // FFN_TF
    nti = s // FFN_TM
    vec = pl.BlockSpec((None, 1, d), lambda b, i, j: (b, 0, 0))
    row = pl.BlockSpec((1, d), lambda b, i, j: (0, 0))
    tile = pl.BlockSpec((None, FFN_TM, d), lambda b, i, j: (b, i, 0))
    in_specs = [tile, vec, vec, vec,
                pl.BlockSpec((d, FFN_TF), lambda b, i, j: (0, j)),
                pl.BlockSpec((d, FFN_TF), lambda b, i, j: (0, nf + j)),
                pl.BlockSpec((FFN_TF, d), lambda b, i, j: (j, 0)),
                row, row]
    out_specs, out_shape = [tile], [jax.ShapeDtypeStruct((bsz, s, d), F32)]
    operands = [x, shift, scale, gate, w_gu, w_gu, w_down,
                ln_g.reshape(1, d), ln_b.reshape(1, d)]
    if cast_weight is not None:
        nrb, ncb = cast_weight.shape[0] // CAST_BLOCK[0], cast_weight.shape[1] // CAST_BLOCK[1]
        assert (nrb * CAST_BLOCK[0], ncb * CAST_BLOCK[1]) == cast_weight.shape
        assert nrb * ncb <= bsz * nti * nf

        def cast_index(b, i, j):
            blk = jnp.minimum((b * nti + i) * nf + j, nrb * ncb - 1)
            return blk // ncb, blk % ncb

        cast_spec = pl.BlockSpec(CAST_BLOCK, cast_index)
        in_specs.append(cast_spec)
        out_specs.append(cast_spec)
        out_shape.append(jax.ShapeDtypeStruct(cast_weight.shape, BF16))
        operands.append(cast_weight)
    outs = pl.pallas_call(
        functools.partial(_ffn_kernel, alpha, cast_weight is not None),
        grid=(bsz, nti, nf),
        in_specs=in_specs,
        out_specs=out_specs,
        out_shape=out_shape,
        scratch_shapes=[pltpu.VMEM((FFN_TM, d), BF16), pltpu.VMEM((FFN_TM, FFN_TF), BF16)],
        compiler_params=_params("arbitrary", "arbitrary", "arbitrary",
                                vmem_limit_bytes=FFN_VMEM_LIMIT_BYTES),
        name="ffn_ln",
    )(*operands)
    return outs if cast_weight is not None else outs[0]


def _gmlp_kernel(n_cast, x_ref, sh_ref, sc_ref, wu_ref, wv_ref, lg_ref, lb_ref, ws_ref,
                 bst_ref, *rest):
    cast_in, o_ref, cast_out = rest[:n_cast], rest[n_cast], rest[n_cast + 1:]
    for src, dst in zip(cast_in, cast_out):
        dst[...] = src[...].astype(BF16)

    pi = lax.broadcasted_iota(jnp.int32, (GMLP_BLOCK, GMLP_BLOCK), 0) // CHUNK
    pj = lax.broadcasted_iota(jnp.int32, (GMLP_BLOCK, GMLP_BLOCK), 1) // CHUNK
    keep = pj <= pi
    bst = bst_ref[...]

    h = (x_ref[...] * (1.0 + sc_ref[...]) + sh_ref[...]).astype(BF16)

    def gelu_proj(w_ref):
        parts = []
        for c in range(w_ref.shape[1] // PROJ_COLS):
            cols = slice(c * PROJ_COLS, (c + 1) * PROJ_COLS)
            parts.append(jax.nn.gelu(jnp.dot(h, w_ref[:, cols], preferred_element_type=F32)))
        return jnp.concatenate(parts, axis=1)

    v = _layer_norm_rows(gelu_proj(wv_ref), lg_ref[...], lb_ref[...]).astype(BF16)
    u = gelu_proj(wu_ref)
    n_blk = x_ref.shape[0] // GMLP_BLOCK
    for g in range(GMLP_GROUPS):
        w = jnp.where(keep, ws_ref[g], 0.0).astype(BF16)
        bias = bst[:, g:g + 1]
        cols = slice(g * LANES, (g + 1) * LANES)
        v_blocks = jnp.concatenate(
            [v[blk * GMLP_BLOCK:(blk + 1) * GMLP_BLOCK, cols] for blk in range(n_blk)], axis=1)
        mixed = jnp.dot(w, v_blocks, preferred_element_type=F32) + bias
        for blk in range(n_blk):
            rows = slice(blk * GMLP_BLOCK, (blk + 1) * GMLP_BLOCK)
            o_ref[rows, cols] = (u[rows, cols]
                                 * mixed[:, blk * LANES:(blk + 1) * LANES]).astype(BF16)


def _gmlp(x, shift, scale, w_u, w_v, ln_g, ln_b, w_s, b_s_t, cast_weights):
    bsz, s, d = x.shape
    width = w_u.shape[1]
    nt = s // MIX_TM
    vec = pl.BlockSpec((None, 1, d), lambda b, i: (b, 0, 0))
    full = lambda shape: pl.BlockSpec(shape, lambda b, i: (0,) * len(shape))
    slab_specs = []
    for w in cast_weights:
        rows = w.shape[0] // (bsz * nt)
        assert rows * bsz * nt == w.shape[0] and rows % 16 == 0, w.shape
        slab_specs.append(pl.BlockSpec((rows, w.shape[1]), lambda b, i: (b * nt + i, 0)))
    return pl.pallas_call(
        functools.partial(_gmlp_kernel, len(cast_weights)),
        grid=(bsz, nt),
        in_specs=[pl.BlockSpec((None, MIX_TM, d), lambda b, i: (b, i, 0)), vec, vec,
                  full((d, width)), full((d, width)),
                  full((1, width)), full((1, width)),
                  full(w_s.shape), full(b_s_t.shape)] + slab_specs,
        out_specs=[pl.BlockSpec((None, MIX_TM, width), lambda b, i: (b, i, 0))] + slab_specs,
        out_shape=[jax.ShapeDtypeStruct((bsz, s, width), BF16)]
        + [jax.ShapeDtypeStruct(w.shape, BF16) for w in cast_weights],
        compiler_params=_params("arbitrary", "arbitrary"),
        name="gmlp",
    )(x, shift, scale, w_u, w_v, ln_g.reshape(1, width), ln_b.reshape(1, width),
      w_s, b_s_t, *cast_weights)


def _rope(x, cos, sin_signed, first_half):
    rot = jnp.where(first_half, pltpu.roll(x, 3 * LANES // 4, 1),
                    pltpu.roll(x, LANES // 4, 1))
    return x * cos + rot * sin_signed


def _qkv_kernel(x_ref, sh_ref, sc_ref, wq_ref, wk_ref, wv_ref, cos_ref, sin_ref,
                qq_ref, k_ref, vt_ref):
    t = x_ref.shape[0]
    h = (x_ref[...] * (1.0 + sc_ref[...]) + sh_ref[...]).astype(BF16)
    cos = cos_ref[...]
    sin = sin_ref[...]
    lane = lax.broadcasted_iota(jnp.int32, (t, LANES), 1)
    first_half = (lane % DIFF_QK_DIM) < (DIFF_QK_DIM // 2)
    comp = lax.broadcasted_iota(jnp.int32, (LANES, t), 0) < DIFF_QK_DIM
    q_scale = DIFF_QK_DIM ** -0.5 * LOG2_E
    ones = jnp.ones((V_AUG_ROWS - DIFF_V_DIM, t), BF16)
    row_chunk = lax.broadcasted_iota(jnp.int32, (t, LANES), 0) // CHUNK
    chunk_onehot = jnp.where(row_chunk == lane, 1.0, 0.0).astype(BF16)
    q = jnp.dot(h, wq_ref[...], preferred_element_type=F32)
    k = jnp.dot(h, wk_ref[...], preferred_element_type=F32)
    v = jnp.dot(h, wv_ref[...], preferred_element_type=F32)
    for hd in range(DIFF_HEADS):
        cols = slice(hd * LANES, (hd + 1) * LANES)
        k_ref[hd, :, :LANES] = _rope(k[:, cols], cos, sin, first_half).astype(BF16)
        k_ref[hd, :, LANES:] = chunk_onehot
        qt = (_rope(q[:, cols], cos, sin, first_half) * q_scale).T
        qq_ref[hd, 0, :, :t] = jnp.where(comp, qt, 0.0).astype(BF16)
        qq_ref[hd, 0, :, t:] = jnp.where(comp, 0.0, qt).astype(BF16)
        vt_ref[hd, 0, :DIFF_V_DIM, :] = v[:, cols].T.astype(BF16)
        vt_ref[hd, 0, DIFF_V_DIM:, :] = ones


def _qkv_rope(x, shift, scale, w_q, w_k, w_v, cos, sin_signed):
    bsz, s, d = x.shape
    t = ATT_T
    nt = s // t
    width = w_q.shape[1]
    vec = pl.BlockSpec((None, 1, d), lambda b, i: (b, 0, 0))
    wspec = pl.BlockSpec((d, width), lambda b, i: (0, 0))
    tab = pl.BlockSpec((t, LANES), lambda b, i: (i, 0))
    return pl.pallas_call(
        _qkv_kernel,
        grid=(bsz, nt),
        in_specs=[pl.BlockSpec((None, t, d), lambda b, i: (b, i, 0)), vec, vec,
                  wspec, wspec, wspec, tab, tab],
        out_specs=[
            pl.BlockSpec((None, DIFF_HEADS, 1, LANES, 2 * t), lambda b, i: (b, 0, i, 0, 0)),
            pl.BlockSpec((None, DIFF_HEADS, t, 2 * LANES), lambda b, i: (b, 0, i, 0)),
            pl.BlockSpec((None, DIFF_HEADS, 1, V_AUG_ROWS, t), lambda b, i: (b, 0, i, 0, 0)),
        ],
        out_shape=[
            jax.ShapeDtypeStruct((bsz, DIFF_HEADS, nt, LANES, 2 * t), BF16),
            jax.ShapeDtypeStruct((bsz, DIFF_HEADS, s, 2 * LANES), BF16),
            jax.ShapeDtypeStruct((bsz, DIFF_HEADS, nt, V_AUG_ROWS, t), BF16),
        ],
        compiler_params=_params("arbitrary", "arbitrary"),
        name="qkv_rope",
    )(x, shift, scale, w_q, w_k, w_v, cos, sin_signed)


def _attn_kernel(lam_init, n_steps, sched_ref, qq_ref, k_ref, vt_ref, bias_ref,
                 lq1_ref, lk1_ref, lq2_ref, lk2_ref, sg_ref, o_ref,
                 m_ref, acc_ref, fin_ref, s0_ref, s1_ref, smax0_ref, smax1_ref):
    t = ATT_T
    m_ref[...] = jnp.full_like(m_ref, NEG_INF)
    acc_ref[...] = jnp.zeros_like(acc_ref)
    buffers = ((s0_ref, smax0_ref), (s1_ref, smax1_ref))

    def produce(n, s_ref, smax_ref):
        qi, kj, dg = sched_ref[0, n], sched_ref[1, n], sched_ref[2, n]
        rhs = jnp.concatenate([qq_ref[qi], bias_ref[dg]], axis=0)
        kblk = k_ref[pl.ds(pl.multiple_of(kj * t, t), t), :]
        s = jnp.dot(kblk, rhs, preferred_element_type=F32)
        s_ref[...] = s
        smax_ref[...] = jnp.max(s, axis=0, keepdims=True)

    def consume(n, s_ref, smax_ref):
        qi, kj = sched_ref[0, n], sched_ref[1, n]
        s = s_ref[...]
        m_prev = jnp.where(kj == 0, NEG_INF, m_ref[...])
        m_new = jnp.maximum(m_prev, smax_ref[...])
        alpha = jnp.exp2(m_prev - m_new)
        p = jnp.exp2(s - m_new).astype(BF16)
        pv = jnp.dot(vt_ref[kj], p, preferred_element_type=F32)
        acc = alpha * acc_ref[...] + pv
        acc_ref[...] = acc
        fin_ref[qi] = acc
        m_ref[...] = m_new

    produce(0, *buffers[0])
    produce(1, *buffers[1])

    def trip(g, carry):
        for u in range(ATT_UNROLL):
            n = g * ATT_UNROLL + u
            consume(n, *buffers[u % 2])
            produce(n + 2, *buffers[u % 2])
        return carry

    lax.fori_loop(0, n_steps // ATT_UNROLL, trip, 0)

    lam = (jnp.exp(jnp.sum(lq1_ref[...] * lk1_ref[...], keepdims=True))
           - jnp.exp(jnp.sum(lq2_ref[...] * lk2_ref[...], keepdims=True))
           + lam_init)
    sub_g = sg_ref[...] * (1.0 - lam_init)

    def finalize(qi, carry):
        acc = fin_ref[qi]
        o = acc[:DIFF_V_DIM, :] / acc[DIFF_V_DIM:DIFF_V_DIM + 1, :]
        o = o[:, :t] - lam * o[:, t:]
        ms = jnp.mean(o * o, axis=0, keepdims=True)
        o = o * lax.rsqrt(ms + LN_EPS) * sub_g
        o_ref[pl.ds(pl.multiple_of(qi * t, t), t), :] = o.T.astype(BF16)
        return carry

    lax.fori_loop(0, fin_ref.shape[0], finalize, 0)


def _attn_schedule(nt):
    steps = [(qi, kj, int(kj == qi)) for qi in range(nt) for kj in range(qi + 1)]
    n_steps = len(steps)
    steps += [(0, 0, 1)] * 2
    return jnp.asarray(steps, jnp.int32).T, n_steps


def _attn_mask_bias(t):
    key_chunk = np.arange(LANES)[:, None]
    q_chunk = (np.arange(2 * t)[None, :] % t) // CHUNK
    hidden = (key_chunk < t // CHUNK) & (key_chunk > q_chunk)
    diag = np.where(hidden, NEG_INF, 0.0).astype(np.float32)
    return jnp.asarray(np.stack([np.zeros_like(diag), diag]), dtype=BF16)


def _diff_attn(qq, k, vt, lq1, lk1, lq2, lk2, sub_g, lam_init):
    bsz, heads, nt, _, _ = qq.shape
    t = ATT_T
    s = nt * t
    sched, n_steps = _attn_schedule(nt)
    assert n_steps % ATT_UNROLL == 0 and ATT_UNROLL % 2 == 0
    lam_spec = pl.BlockSpec((1, DIFF_QK_DIM), lambda b, h, sc: (0, 0))
    grid_spec = pltpu.PrefetchScalarGridSpec(
        num_scalar_prefetch=1,
        grid=(bsz, heads),
        in_specs=[
            pl.BlockSpec((None, None, nt, LANES, 2 * t), lambda b, h, sc: (b, h, 0, 0, 0)),
            pl.BlockSpec((None, None, s, 2 * LANES), lambda b, h, sc: (b, h, 0, 0)),
            pl.BlockSpec((None, None, nt, V_AUG_ROWS, t), lambda b, h, sc: (b, h, 0, 0, 0)),
            pl.BlockSpec((2, LANES, 2 * t), lambda b, h, sc: (0, 0, 0)),
            lam_spec, lam_spec, lam_spec, lam_spec,
            pl.BlockSpec((DIFF_V_DIM, 1), lambda b, h, sc: (0, 0)),
        ],
        out_specs=pl.BlockSpec((None, s, LANES), lambda b, h, sc: (b, 0, h)),
        scratch_shapes=[pltpu.VMEM((1, 2 * t), F32),
                        pltpu.VMEM((V_AUG_ROWS, 2 * t), F32),
                        pltpu.VMEM((nt, V_AUG_ROWS, 2 * t), F32),
                        pltpu.VMEM((t, 2 * t), F32), pltpu.VMEM((t, 2 * t), F32),
                        pltpu.VMEM((1, 2 * t), F32), pltpu.VMEM((1, 2 * t), F32)],
    )
    return pl.pallas_call(
        functools.partial(_attn_kernel, lam_init, n_steps),
        grid_spec=grid_spec,
        out_shape=jax.ShapeDtypeStruct((bsz, s, heads * DIFF_V_DIM), BF16),
        compiler_params=_params("arbitrary", "arbitrary"),
        name="diff_attn",
    )(sched, qq, k, vt, _attn_mask_bias(t), lq1.reshape(1, -1), lk1.reshape(1, -1),
      lq2.reshape(1, -1), lk2.reshape(1, -1), sub_g.reshape(-1, 1))


def _out_kernel(alpha, x_ref, gt_ref, a_ref, b_ref, wa_ref, wb_ref, lg_ref, lb_ref,
                o_ref):
    gate = 1.0 + gt_ref[...]
    for c in range(x_ref.shape[0] // ROW_CHUNK):
        rows = pl.ds(c * ROW_CHUNK, ROW_CHUNK)
        y = jnp.dot(a_ref[rows, :], wa_ref[...], preferred_element_type=F32)
        y = y + jnp.dot(b_ref[rows, :], wb_ref[...], preferred_element_type=F32)
        z = alpha * x_ref[rows, :] + gate * y
        o_ref[rows, :] = _layer_norm_rows(z, lg_ref[...], lb_ref[...])


def _out_ln(x, gate, out_a, out_b, w_out, ln_g, ln_b, alpha):
    bsz, s, d = x.shape
    half = out_a.shape[-1]
    tile = pl.BlockSpec((None, MIX_TM, d), lambda b, i: (b, i, 0))
    act = pl.BlockSpec((None, MIX_TM, half), lambda b, i: (b, i, 0))
    row = pl.BlockSpec((1, d), lambda b, i: (0, 0))
    return pl.pallas_call(
        functools.partial(_out_kernel, alpha),
        grid=(bsz, s // MIX_TM),
        in_specs=[tile, pl.BlockSpec((None, 1, d), lambda b, i: (b, 0, 0)), act, act,
                  pl.BlockSpec((half, d), lambda b, i: (0, 0)),
                  pl.BlockSpec((half, d), lambda b, i: (1, 0)),
                  row, row],
        out_specs=tile,
        out_shape=jax.ShapeDtypeStruct((bsz, s, d), F32),
        compiler_params=_params("arbitrary", "arbitrary"),
        name="out_ln",
    )(x, gate, out_a, out_b, w_out, w_out, ln_g.reshape(1, d), ln_b.reshape(1, d))


def _rope_tables(s):
    half = DIFF_QK_DIM // 2
    pos = np.arange(s, dtype=np.float64)
    inv_freq = ROPE_THETA ** (-np.arange(0, DIFF_QK_DIM, 2, dtype=np.float64) / DIFF_QK_DIM)
    ang = pos[:, None] * inv_freq[None, :]
    cos, sin = np.cos(ang), np.sin(ang)
    cos = np.concatenate([cos, cos, cos, cos], axis=-1)
    sin_signed = np.concatenate([-sin, sin, -sin, sin], axis=-1)
    assert cos.shape[-1] == 4 * half == LANES
    return jnp.asarray(cos, dtype=F32), jnp.asarray(sin_signed, dtype=F32)


def kernel(x, c, w_ada, b_ada, ffn1_w_gu, ffn1_w_down, ln1_g, ln1_b, w_in, gmlp_ln_g, gmlp_ln_b, gmlp_w_s, gmlp_b_s, lambda_q1, lambda_k1, lambda_q2, lambda_k2, diff_subln_g, w_out, ln2_g, ln2_b, ffn2_w_gu, ffn2_w_down, ln3_g, ln3_b):
    bsz, s, d = x.shape
    depth = w_ada.shape[0]
    alpha = (2 * depth) ** 0.25
    gw = gmlp_ln_g.shape[-1]
    qk = DIFF_HEADS * 2 * DIFF_QK_DIM
    c1, c2, c3, c4 = gw, 2 * gw, 2 * gw + qk, 2 * gw + 2 * qk
    cos, sin_signed = _rope_tables(s)
    c_pad = jnp.zeros((8, d), F32).at[:bsz].set(c)

    for l in range(depth):
        mod = _adaln(c_pad, w_ada[l], b_ada[l])[:bsz]
        mod = mod.reshape(bsz, N_MOD, 1, d)
        s1, sc1, g1, s2, sc2, g2, s3, sc3, g3 = [mod[:, n] for n in range(N_MOD)]

        x, w_in_b = _ffn_ln(x, s1, sc1, g1, ffn1_w_gu[l].astype(BF16),
                            ffn1_w_down[l].astype(BF16), ln1_g[l], ln1_b[l], alpha,
                            cast_weight=w_in[l])
        out_a, w_out_b, w_gu2_b, w_down2_b = _gmlp(
            x, s2, sc2, w_in_b[:, :c1], w_in_b[:, c1:c2], gmlp_ln_g[l], gmlp_ln_b[l],
            gmlp_w_s[l], jnp.transpose(gmlp_b_s[l]),
            cast_weights=(w_out[l], ffn2_w_gu[l], ffn2_w_down[l]))
        qq, k, vt = _qkv_rope(x, s2, sc2, w_in_b[:, c2:c3], w_in_b[:, c3:c4],
                              w_in_b[:, c4:], cos, sin_signed)
        lam_init = 0.8 - 0.6 * math.exp(-0.3 * l)
        out_b = _diff_attn(qq, k, vt, lambda_q1[l], lambda_k1[l], lambda_q2[l],
                           lambda_k2[l], diff_subln_g[l], lam_init)
        x = _out_ln(x, g2, out_a, out_b, w_out_b, ln2_g[l], ln2_b[l], alpha)

        x = _ffn_ln(x, s3, sc3, g3, w_gu2_b, w_down2_b, ln3_g[l], ln3_b[l], alpha)
    return x
```

```python
import functools
import math

import jax
import jax.numpy as jnp
import numpy as np
from jax import lax
from jax.experimental import pallas as pl
from jax.experimental.pallas import tpu as pltpu

F32 = jnp.float32
BF16 = jnp.bfloat16

CHUNK = 64
GMLP_BLOCK = 128
GMLP_GROUPS = 8
DIFF_HEADS = 8
DIFF_V_DIM = 128
DIFF_QK_DIM = 64
N_MOD = 9
ROPE_THETA = 10000.0
LN_EPS = 1e-5
NEG_INF = -1e30
LOG2_E = 1.4426950408889634
V_AUG_ROWS = DIFF_V_DIM + 16

LANES = 128
VMEM_LIMIT_BYTES = 56 * 1024 * 1024
FFN_VMEM_LIMIT_BYTES = 61 * 1024 * 1024

ADA_TN = 1024
FFN_TM = 1024
FFN_TF = 512
FFN_UP_COLS = 256
FFN_DOWN_COLS = 256
CAST_BLOCK = (128, 512)
MIX_TM = 512
ROW_CHUNK = 256
PROJ_COLS = 256
ATT_T = 512
ATT_UNROLL = 34


def _params(*sem, vmem_limit_bytes=VMEM_LIMIT_BYTES):
    return pltpu.CompilerParams(dimension_semantics=sem,
                                vmem_limit_bytes=vmem_limit_bytes)


def _layer_norm_rows(z, g, b):
    mu = jnp.mean(z, axis=-1, keepdims=True)
    zc = z - mu
    var = jnp.mean(zc * zc, axis=-1, keepdims=True)
    return zc * lax.rsqrt(var + LN_EPS) * g + b


def _adaln_block(c_ref, w_ref, b_ref):
    c = c_ref[...]
    ca = (c / (1.0 + jnp.exp(-c))).astype(BF16)
    return jnp.dot(ca, w_ref[...].astype(BF16), preferred_element_type=F32) + b_ref[...]


def _adaln_kernel(c_ref, w_ref, b_ref, o_ref):
    o_ref[...] = _adaln_block(c_ref, w_ref, b_ref)


def _adaln(c_pad, w, b, n_cols):
    d = w.shape[0]
    return pl.pallas_call(
        _adaln_kernel,
        grid=(n_cols // ADA_TN,),
        in_specs=[pl.BlockSpec((8, d), lambda j: (0, 0)),
                  pl.BlockSpec((d, ADA_TN), lambda j: (0, j)),
                  pl.BlockSpec((1, ADA_TN), lambda j: (0, j))],
        out_specs=pl.BlockSpec((8, ADA_TN), lambda j: (0, j)),
        out_shape=jax.ShapeDtypeStruct((8, n_cols), F32),
        compiler_params=_params("arbitrary"),
        name="adaln",
    )(c_pad, w, b.reshape(1, -1))


def _ffn_kernel(alpha, has_cast, x_ref, sh_ref, sc_ref, gt_ref, wg_ref, wu_ref, wd_ref,
                lg_ref, lb_ref, *rest):
    if has_cast:
        cast_in, o_ref, cast_out, h_ref, a_ref = rest
        cast_out[...] = cast_in[...].astype(BF16)
    else:
        o_ref, h_ref, a_ref = rest
    j = pl.program_id(2)
    last = pl.num_programs(2) - 1

    def hidden():
        h = h_ref[...]
        for c in range(a_ref.shape[1] // FFN_UP_COLS):
            cols = slice(c * FFN_UP_COLS, (c + 1) * FFN_UP_COLS)
            g = jnp.dot(h, wg_ref[:, cols], preferred_element_type=F32)
            u = jnp.dot(h, wu_ref[:, cols], preferred_element_type=F32)
            a_ref[:, cols] = (g / (1.0 + jnp.exp(-g)) * u).astype(BF16)

    def down(first):
        a = a_ref[...]
        for c in range(o_ref.shape[1] // FFN_DOWN_COLS):
            cols = slice(c * FFN_DOWN_COLS, (c + 1) * FFN_DOWN_COLS)
            y = jnp.dot(a, wd_ref[:, cols], preferred_element_type=F32)
            o_ref[:, cols] = y if first else o_ref[:, cols] + y

    @pl.when(j == 0)
    def _():
        h_ref[...] = (x_ref[...] * (1.0 + sc_ref[...]) + sh_ref[...]).astype(BF16)
        hidden()
        down(first=True)

    @pl.when(jnp.logical_and(j > 0, j < last))
    def _():
        hidden()
        down(first=False)

    @pl.when(j == last)
    def _():
        hidden()
        gate = 0.5 * (1.0 + gt_ref[...])
        for r in range(o_ref.shape[0] // ROW_CHUNK):
            rows = pl.ds(r * ROW_CHUNK, ROW_CHUNK)
            y = o_ref[rows, :] + jnp.dot(a_ref[rows, :], wd_ref[...],
                                         preferred_element_type=F32)
            z = alpha * x_ref[rows, :] + gate * y
            o_ref[rows, :] = _layer_norm_rows(z, lg_ref[...], lb_ref[...])


def _ffn_ln(x, shift, scale, gate, w_gu, w_down, ln_g, ln_b, alpha, cast_weight=None):
    bsz, s, d = x.shape
    f = w_down.shape[0]
    nf = f // FFN_TF
    nti = s // FFN_TM
    vec = pl.BlockSpec((None, 1, d), lambda b, i, j: (b, 0, 0))
    row = pl.BlockSpec((1, d), lambda b, i, j: (0, 0))
    tile = pl.BlockSpec((None, FFN_TM, d), lambda b, i, j: (b, i, 0))
    in_specs = [tile, vec, vec, vec,
                pl.BlockSpec((d, FFN_TF), lambda b, i, j: (0, j)),
                pl.BlockSpec((d, FFN_TF), lambda b, i, j: (0, nf + j)),
                pl.BlockSpec((FFN_TF, d), lambda b, i, j: (j, 0)),
                row, row]
    out_specs, out_shape = [tile], [jax.ShapeDtypeStruct((bsz, s, d), F32)]
    operands = [x, shift, scale, gate, w_gu, w_gu, w_down,
                ln_g.reshape(1, d), ln_b.reshape(1, d)]
    if cast_weight is not None:
        nrb, ncb = cast_weight.shape[0] // CAST_BLOCK[0], cast_weight.shape[1] // CAST_BLOCK[1]
        assert (nrb * CAST_BLOCK[0], ncb * CAST_BLOCK[1]) == cast_weight.shape
        assert nrb * ncb <= bsz * nti * nf

        def cast_index(b, i, j):
            blk = jnp.minimum((b * nti + i) * nf + j, nrb * ncb - 1)
            return blk // ncb, blk % ncb

        cast_spec = pl.BlockSpec(CAST_BLOCK, cast_index)
        in_specs.append(cast_spec)
        out_specs.append(cast_spec)
        out_shape.append(jax.ShapeDtypeStruct(cast_weight.shape, BF16))
        operands.append(cast_weight)
    outs = pl.pallas_call(
        functools.partial(_ffn_kernel, alpha, cast_weight is not None),
        grid=(bsz, nti, nf),
        in_specs=in_specs,
        out_specs=out_specs,
        out_shape=out_shape,
        scratch_shapes=[pltpu.VMEM((FFN_TM, d), BF16), pltpu.VMEM((FFN_TM, FFN_TF), BF16)],
        compiler_params=_params("arbitrary", "arbitrary", "arbitrary",
                                vmem_limit_bytes=FFN_VMEM_LIMIT_BYTES),
        name="ffn_ln",
    )(*operands)
    return outs if cast_weight is not None else outs[0]


def _gmlp_kernel(n_cast, x_ref, sh_ref, sc_ref, wu_ref, wv_ref, lg_ref, lb_ref, ws_ref,
                 bst_ref, *rest):
    cast_in, o_ref, cast_out = rest[:n_cast], rest[n_cast], rest[n_cast + 1:]
    for src, dst in zip(cast_in, cast_out):
        dst[...] = src[...].astype(BF16)

    pi = lax.broadcasted_iota(jnp.int32, (GMLP_BLOCK, GMLP_BLOCK), 0) // CHUNK
    pj = lax.broadcasted_iota(jnp.int32, (GMLP_BLOCK, GMLP_BLOCK), 1) // CHUNK
    keep = pj <= pi
    bst = bst_ref[...]

    h = (x_ref[...] * (1.0 + sc_ref[...]) + sh_ref[...]).astype(BF16)

    def gelu_proj(w_ref):
        parts = []
        for c in range(w_ref.shape[1] // PROJ_COLS):
            cols = slice(c * PROJ_COLS, (c + 1) * PROJ_COLS)
            parts.append(jax.nn.gelu(jnp.dot(h, w_ref[:, cols], preferred_element_type=F32)))
        return jnp.concatenate(parts, axis=1)

    v = _layer_norm_rows(gelu_proj(wv_ref), lg_ref[...], lb_ref[...]).astype(BF16)
    u = gelu_proj(wu_ref)
    n_blk = x_ref.shape[0] // GMLP_BLOCK
    for g in range(GMLP_GROUPS):
        w = jnp.where(keep, ws_ref[g], 0.0).astype(BF16)
        bias = bst[:, g:g + 1]
        cols = slice(g * LANES, (g + 1) * LANES)
        v_blocks = jnp.concatenate(
            [v[blk * GMLP_BLOCK:(blk + 1) * GMLP_BLOCK, cols] for blk in range(n_blk)], axis=1)
        mixed = jnp.dot(w, v_blocks, preferred_element_type=F32) + bias
        for blk in range(n_blk):
            rows = slice(blk * GMLP_BLOCK, (blk + 1) * GMLP_BLOCK)
            o_ref[rows, cols] = (u[rows, cols]
                                 * mixed[:, blk * LANES:(blk + 1) * LANES]).astype(BF16)


def _gmlp(x, shift, scale, w_in, width, ln_g, ln_b, w_s, b_s_t, cast_weights):
    bsz, s, d = x.shape
    nt = s // MIX_TM
    vec = pl.BlockSpec((None, 1, d), lambda b, i: (b, 0, 0))
    full = lambda shape: pl.BlockSpec(shape, lambda b, i: (0,) * len(shape))
    w_cols = lambda blk: pl.BlockSpec((d, width), lambda b, i: (0, blk))
    slab_specs = []
    for w in cast_weights:
        rows = w.shape[0] // (bsz * nt)
        assert rows * bsz * nt == w.shape[0] and rows % 16 == 0, w.shape
        slab_specs.append(pl.BlockSpec((rows, w.shape[1]), lambda b, i: (b * nt + i, 0)))
    return pl.pallas_call(
        functools.partial(_gmlp_kernel, len(cast_weights)),
        grid=(bsz, nt),
        in_specs=[pl.BlockSpec((None, MIX_TM, d), lambda b, i: (b, i, 0)), vec, vec,
                  w_cols(0), w_cols(1),
                  full((1, width)), full((1, width)),
                  full(w_s.shape), full(b_s_t.shape)] + slab_specs,
        out_specs=[pl.BlockSpec((None, MIX_TM, width), lambda b, i: (b, i, 0))] + slab_specs,
        out_shape=[jax.ShapeDtypeStruct((bsz, s, width), BF16)]
        + [jax.ShapeDtypeStruct(w.shape, BF16) for w in cast_weights],
        compiler_params=_params("arbitrary", "arbitrary"),
        name="gmlp",
    )(x, shift, scale, w_in, w_in, ln_g.reshape(1, width), ln_b.reshape(1, width),
      w_s, b_s_t, *cast_weights)


def _rope(x, cos, sin_signed, first_half):
    rot = jnp.where(first_half, pltpu.roll(x, 3 * LANES // 4, 1),
                    pltpu.roll(x, LANES // 4, 1))
    return x * cos + rot * sin_signed


def _qkv_kernel(x_ref, sh_ref, sc_ref, wq_ref, wk_ref, wv_ref, cos_ref, sin_ref,
                qq_ref, k_ref, vt_ref):
    t = x_ref.shape[0]
    h = (x_ref[...] * (1.0 + sc_ref[...]) + sh_ref[...]).astype(BF16)
    cos = cos_ref[...]
    sin = sin_ref[...]
    lane = lax.broadcasted_iota(jnp.int32, (t, LANES), 1)
    first_half = (lane % DIFF_QK_DIM) < (DIFF_QK_DIM // 2)
    comp = lax.broadcasted_iota(jnp.int32, (LANES, t), 0) < DIFF_QK_DIM
    q_scale = DIFF_QK_DIM ** -0.5 * LOG2_E
    ones = jnp.ones((V_AUG_ROWS - DIFF_V_DIM, t), BF16)
    row_chunk = lax.broadcasted_iota(jnp.int32, (t, LANES), 0) // CHUNK
    chunk_onehot = jnp.where(row_chunk == lane, 1.0, 0.0).astype(BF16)
    q = jnp.dot(h, wq_ref[...], preferred_element_type=F32)
    k = jnp.dot(h, wk_ref[...], preferred_element_type=F32)
    v = jnp.dot(h, wv_ref[...], preferred_element_type=F32)
    for hd in range(DIFF_HEADS):
        cols = slice(hd * LANES, (hd + 1) * LANES)
        k_ref[hd, :, :LANES] = _rope(k[:, cols], cos, sin, first_half).astype(BF16)
        k_ref[hd, :, LANES:] = chunk_onehot
        qt = (_rope(q[:, cols], cos, sin, first_half) * q_scale).T
        qq_ref[hd, 0, :, :t] = jnp.where(comp, qt, 0.0).astype(BF16)
        qq_ref[hd, 0, :, t:] = jnp.where(comp, 0.0, qt).astype(BF16)
        vt_ref[hd, 0, :DIFF_V_DIM, :] = v[:, cols].T.astype(BF16)
        vt_ref[hd, 0, DIFF_V_DIM:, :] = ones


def _qkv_rope(x, shift, scale, w_in, width, cos, sin_signed):
    bsz, s, d = x.shape
    t = ATT_T
    nt = s // t
    vec = pl.BlockSpec((None, 1, d), lambda b, i: (b, 0, 0))
    w_cols = lambda blk: pl.BlockSpec((d, width), lambda b, i: (0, blk))
    tab = pl.BlockSpec((t, LANES), lambda b, i: (i, 0))
    return pl.pallas_call(
        _qkv_kernel,
        grid=(bsz, nt),
        in_specs=[pl.BlockSpec((None, t, d), lambda b, i: (b, i, 0)), vec, vec,
                  w_cols(2), w_cols(3), w_cols(4), tab, tab],
        out_specs=[
            pl.BlockSpec((None, DIFF_HEADS, 1, LANES, 2 * t), lambda b, i: (b, 0, i, 0, 0)),
            pl.BlockSpec((None, DIFF_HEADS, t, 2 * LANES), lambda b, i: (b, 0, i, 0)),
            pl.BlockSpec((None, DIFF_HEADS, 1, V_AUG_ROWS, t), lambda b, i: (b, 0, i, 0, 0)),
        ],
        out_shape=[
            jax.ShapeDtypeStruct((bsz, DIFF_HEADS, nt, LANES, 2 * t), BF16),
            jax.ShapeDtypeStruct((bsz, DIFF_HEADS, s, 2 * LANES), BF16),
            jax.ShapeDtypeStruct((bsz, DIFF_HEADS, nt, V_AUG_ROWS, t), BF16),
        ],
        compiler_params=_params("arbitrary", "arbitrary"),
        name="qkv_rope",
    )(x, shift, scale, w_in, w_in, w_in, cos, sin_signed)


def _attn_kernel(lam_init, n_steps, sched_ref, qq_ref, k_ref, vt_ref, bias_ref,
                 lq1_ref, lk1_ref, lq2_ref, lk2_ref, sg_ref, c_ref, wada_ref, bada_ref,
                 o_ref, mod_ref,
                 m_ref, acc_ref, fin_ref, s0_ref, s1_ref, smax0_ref, smax1_ref):
    t = ATT_T
    mod_ref[...] = _adaln_block(c_ref, wada_ref, bada_ref)

    m_ref[...] = jnp.full_like(m_ref, NEG_INF)
    acc_ref[...] = jnp.zeros_like(acc_ref)
    buffers = ((s0_ref, smax0_ref), (s1_ref, smax1_ref))

    def produce(n, s_ref, smax_ref):
        qi, kj, dg = sched_ref[0, n], sched_ref[1, n], sched_ref[2, n]
        rhs = jnp.concatenate([qq_ref[qi], bias_ref[dg]], axis=0)
        kblk = k_ref[pl.ds(pl.multiple_of(kj * t, t), t), :]
        s = jnp.dot(kblk, rhs, preferred_element_type=F32)
        s_ref[...] = s
        smax_ref[...] = jnp.max(s, axis=0, keepdims=True)

    def consume(n, s_ref, smax_ref):
        qi, kj = sched_ref[0, n], sched_ref[1, n]
        s = s_ref[...]
        m_prev = jnp.where(kj == 0, NEG_INF, m_ref[...])
        m_new = jnp.maximum(m_prev, smax_ref[...])
        alpha = jnp.exp2(m_prev - m_new)
        p = jnp.exp2(s - m_new).astype(BF16)
        pv = jnp.dot(vt_ref[kj], p, preferred_element_type=F32)
        acc = alpha * acc_ref[...] + pv
        acc_ref[...] = acc
        fin_ref[qi] = acc
        m_ref[...] = m_new

    produce(0, *buffers[0])
    produce(1, *buffers[1])

    def trip(g, carry):
        for u in range(ATT_UNROLL):
            n = g * ATT_UNROLL + u
            consume(n, *buffers[u % 2])
            produce(n + 2, *buffers[u % 2])
        return carry

    lax.fori_loop(0, n_steps // ATT_UNROLL, trip, 0)

    lam = (jnp.exp(jnp.sum(lq1_ref[...] * lk1_ref[...], keepdims=True))
           - jnp.exp(jnp.sum(lq2_ref[...] * lk2_ref[...], keepdims=True))
           + lam_init)
    sub_g = sg_ref[...] * (1.0 - lam_init)

    def finalize(qi, carry):
        acc = fin_ref[qi]
        o = acc[:DIFF_V_DIM, :] / acc[DIFF_V_DIM:DIFF_V_DIM + 1, :]
        o = o[:, :t] - lam * o[:, t:]
        ms = jnp.mean(o * o, axis=0, keepdims=True)
        o = o * lax.rsqrt(ms + LN_EPS) * sub_g
        o_ref[pl.ds(pl.multiple_of(qi * t, t), t), :] = o.T.astype(BF16)
        return carry

    lax.fori_loop(0, fin_ref.shape[0], finalize, 0)


def _attn_schedule(nt):
    steps = [(qi, kj, int(kj == qi)) for qi in range(nt) for kj in range(qi + 1)]
    n_steps = len(steps)
    steps += [(0, 0, 1)] * 2
    return jnp.asarray(steps, jnp.int32).T, n_steps


def _attn_mask_bias(t):
    key_chunk = np.arange(LANES)[:, None]
    q_chunk = (np.arange(2 * t)[None, :] % t) // CHUNK
    hidden = (key_chunk < t // CHUNK) & (key_chunk > q_chunk)
    diag = np.where(hidden, NEG_INF, 0.0).astype(np.float32)
    return jnp.asarray(np.stack([np.zeros_like(diag), diag]), dtype=BF16)


def _diff_attn(qq, k, vt, lq1, lk1, lq2, lk2, sub_g, lam_init, c_pad, w_ada, b_ada, ada_col0):
    bsz, heads, nt, _, _ = qq.shape
    t = ATT_T
    s = nt * t
    sched, n_steps = _attn_schedule(nt)
    assert n_steps % ATT_UNROLL == 0 and ATT_UNROLL % 2 == 0
    n_tail = w_ada.shape[1] - ada_col0
    ada_tn = n_tail // (bsz * heads)
    assert ada_tn * bsz * heads == n_tail and ada_tn % LANES == 0 and ada_col0 % ada_tn == 0
    ada_blk0 = ada_col0 // ada_tn
    lam_spec = pl.BlockSpec((1, DIFF_QK_DIM), lambda b, h, sc: (0, 0))
    grid_spec = pltpu.PrefetchScalarGridSpec(
        num_scalar_prefetch=1,
        grid=(bsz, heads),
        in_specs=[
            pl.BlockSpec((None, None, nt, LANES, 2 * t), lambda b, h, sc: (b, h, 0, 0, 0)),
            pl.BlockSpec((None, None, s, 2 * LANES), lambda b, h, sc: (b, h, 0, 0)),
            pl.BlockSpec((None, None, nt, V_AUG_ROWS, t), lambda b, h, sc: (b, h, 0, 0, 0)),
            pl.BlockSpec((2, LANES, 2 * t), lambda b, h, sc: (0, 0, 0)),
            lam_spec, lam_spec, lam_spec, lam_spec,
            pl.BlockSpec((DIFF_V_DIM, 1), lambda b, h, sc: (0, 0)),
            pl.BlockSpec(c_pad.shape, lambda b, h, sc: (0, 0)),
            pl.BlockSpec((w_ada.shape[0], ada_tn),
                         lambda b, h, sc: (0, ada_blk0 + b * heads + h)),
            pl.BlockSpec((1, ada_tn), lambda b, h, sc: (0, ada_blk0 + b * heads + h)),
        ],
        out_specs=[pl.BlockSpec((None, s, LANES), lambda b, h, sc: (b, 0, h)),
                   pl.BlockSpec((8, ada_tn), lambda b, h, sc: (0, b * heads + h))],
        scratch_shapes=[pltpu.VMEM((1, 2 * t), F32),
                        pltpu.VMEM((V_AUG_ROWS, 2 * t), F32),
                        pltpu.VMEM((nt, V_AUG_ROWS, 2 * t), F32),
                        pltpu.VMEM((t, 2 * t), F32), pltpu.VMEM((t, 2 * t), F32),
                        pltpu.VMEM((1, 2 * t), F32), pltpu.VMEM((1, 2 * t), F32)],
    )
    return pl.pallas_call(
        functools.partial(_attn_kernel, lam_init, n_steps),
        grid_spec=grid_spec,
        out_shape=[jax.ShapeDtypeStruct((bsz, s, heads * DIFF_V_DIM), BF16),
                   jax.ShapeDtypeStruct((8, n_tail), F32)],
        compiler_params=_params("arbitrary", "arbitrary"),
        name="diff_attn",
    )(sched, qq, k, vt, _attn_mask_bias(t), lq1.reshape(1, -1), lk1.reshape(1, -1),
      lq2.reshape(1, -1), lk2.reshape(1, -1), sub_g.reshape(-1, 1),
      c_pad, w_ada, b_ada.reshape(1, -1))


def _out_kernel(alpha, x_ref, gt_ref, a_ref, b_ref, wa_ref, wb_ref, lg_ref, lb_ref,
                o_ref):
    gate = 1.0 + gt_ref[...]
    for c in range(x_ref.shape[0] // ROW_CHUNK):
        rows = pl.ds(c * ROW_CHUNK, ROW_CHUNK)
        y = jnp.dot(a_ref[rows, :], wa_ref[...], preferred_element_type=F32)
        y = y + jnp.dot(b_ref[rows, :], wb_ref[...], preferred_element_type=F32)
        z = alpha * x_ref[rows, :] + gate * y
        o_ref[rows, :] = _layer_norm_rows(z, lg_ref[...], lb_ref[...])


def _out_ln(x, gate, out_a, out_b, w_out, ln_g, ln_b, alpha):
    bsz, s, d = x.shape
    half = out_a.shape[-1]
    tile = pl.BlockSpec((None, MIX_TM, d), lambda b, i: (b, i, 0))
    act = pl.BlockSpec((None, MIX_TM, half), lambda b, i: (b, i, 0))
    row = pl.BlockSpec((1, d), lambda b, i: (0, 0))
    return pl.pallas_call(
        functools.partial(_out_kernel, alpha),
        grid=(bsz, s // MIX_TM),
        in_specs=[tile, pl.BlockSpec((None, 1, d), lambda b, i: (b, 0, 0)), act, act,
                  pl.BlockSpec((half, d), lambda b, i: (0, 0)),
                  pl.BlockSpec((half, d), lambda b, i: (1, 0)),
                  row, row],
        out_specs=tile,
        out_shape=jax.ShapeDtypeStruct((bsz, s, d), F32),
        compiler_params=_params("arbitrary", "arbitrary"),
        name="out_ln",
    )(x, gate, out_a, out_b, w_out, w_out, ln_g.reshape(1, d), ln_b.reshape(1, d))


def _rope_tables(s):
    half = DIFF_QK_DIM // 2
    pos = np.arange(s, dtype=np.float64)
    inv_freq = ROPE_THETA ** (-np.arange(0, DIFF_QK_DIM, 2, dtype=np.float64) / DIFF_QK_DIM)
    ang = pos[:, None] * inv_freq[None, :]
    cos, sin = np.cos(ang), np.sin(ang)
    cos = np.concatenate([cos, cos, cos, cos], axis=-1)
    sin_signed = np.concatenate([-sin, sin, -sin, sin], axis=-1)
    assert cos.shape[-1] == 4 * half == LANES
    return jnp.asarray(cos, dtype=F32), jnp.asarray(sin_signed, dtype=F32)


def kernel(x, c, w_ada, b_ada, ffn1_w_gu, ffn1_w_down, ln1_g, ln1_b, w_in, gmlp_ln_g, gmlp_ln_b, gmlp_w_s, gmlp_b_s, lambda_q1, lambda_k1, lambda_q2, lambda_k2, diff_subln_g, w_out, ln2_g, ln2_b, ffn2_w_gu, ffn2_w_down, ln3_g, ln3_b):
    bsz, s, d = x.shape
    depth = w_ada.shape[0]
    alpha = (2 * depth) ** 0.25
    gw = gmlp_ln_g.shape[-1]
    qk = DIFF_HEADS * 2 * DIFF_QK_DIM
    cos, sin_signed = _rope_tables(s)
    c_pad = jnp.zeros((8, d), F32).at[:bsz].set(c)

    assert qk == gw and w_in.shape[-1] == 5 * gw
    n_early = 5

    for l in range(depth):
        mod_early = _adaln(c_pad, w_ada[l], b_ada[l], n_early * d)[:bsz]
        s1, sc1, g1, s2, sc2 = [m[:, None, :] for m in jnp.split(mod_early, n_early, axis=-1)]

        x, w_in_b = _ffn_ln(x, s1, sc1, g1, ffn1_w_gu[l].astype(BF16),
                            ffn1_w_down[l].astype(BF16), ln1_g[l], ln1_b[l], alpha,
                            cast_weight=w_in[l])
        out_a, w_out_b, w_gu2_b, w_down2_b = _gmlp(
            x, s2, sc2, w_in_b, gw, gmlp_ln_g[l], gmlp_ln_b[l],
            gmlp_w_s[l], jnp.transpose(gmlp_b_s[l]),
            cast_weights=(w_out[l], ffn2_w_gu[l], ffn2_w_down[l]))
        qq, k, vt = _qkv_rope(x, s2, sc2, w_in_b, gw, cos, sin_signed)
        lam_init = 0.8 - 0.6 * math.exp(-0.3 * l)
        out_b, mod_late = _diff_attn(qq, k, vt, lambda_q1[l], lambda_k1[l], lambda_q2[l],
                                     lambda_k2[l], diff_subln_g[l], lam_init,
                                     c_pad, w_ada[l], b_ada[l], n_early * d)
        g2, s3, sc3, g3 = [m[:, None, :]
                           for m in jnp.split(mod_late[:bsz], N_MOD - n_early, axis=-1)]
        x = _out_ln(x, g2, out_a, out_b, w_out_b, ln2_g[l], ln2_b[l], alpha)

        x = _ffn_ln(x, s3, sc3, g3, w_gu2_b, w_down2_b, ln3_g[l], ln3_b[l], alpha)
    return x
```

```python
import functools
import math

import jax
import jax.numpy as jnp
import numpy as np
from jax import lax
from jax.experimental import pallas as pl
from jax.experimental.pallas import tpu as pltpu

F32 = jnp.float32
BF16 = jnp.bfloat16

CHUNK = 64
GMLP_BLOCK = 128
GMLP_GROUPS = 8
DIFF_HEADS = 8
DIFF_V_DIM = 128
DIFF_QK_DIM = 64
N_MOD = 9
ROPE_THETA = 10000.0
LN_EPS = 1e-5
NEG_INF = -1e30
LOG2_E = 1.4426950408889634
V_AUG_ROWS = DIFF_V_DIM + 16

LANES = 128
VMEM_LIMIT_BYTES = 56 * 1024 * 1024
FFN_VMEM_LIMIT_BYTES = 61 * 1024 * 1024

ADA_TN = 1024
FFN_TM = 1024
FFN_TF = 512
FFN_TF_F32 = 256
FFN_UP_COLS = 256
FFN_DOWN_COLS = 256
CAST_BLOCK = (128, 512)
MIX_TM = 512
ROW_CHUNK = 256
PROJ_COLS = 256
ATT_T = 512
ATT_UNROLL = 34


def _params(*sem, vmem_limit_bytes=VMEM_LIMIT_BYTES):
    return pltpu.CompilerParams(dimension_semantics=sem,
                                vmem_limit_bytes=vmem_limit_bytes)


def _layer_norm_rows(z, g, b):
    mu = jnp.mean(z, axis=-1, keepdims=True)
    zc = z - mu
    var = jnp.mean(zc * zc, axis=-1, keepdims=True)
    return zc * lax.rsqrt(var + LN_EPS) * g + b


def _adaln_block(c_ref, w_ref, b_ref):
    c = c_ref[...]
    ca = (c / (1.0 + jnp.exp(-c))).astype(BF16)
    return jnp.dot(ca, w_ref[...].astype(BF16), preferred_element_type=F32) + b_ref[...]


def _adaln_kernel(c_ref, w_ref, b_ref, o_ref):
    o_ref[...] = _adaln_block(c_ref, w_ref, b_ref)


def _adaln(c_pad, w, b, n_cols):
    d = w.shape[0]
    return pl.pallas_call(
        _adaln_kernel,
        grid=(n_cols // ADA_TN,),
        in_specs=[pl.BlockSpec((8, d), lambda j: (0, 0)),
                  pl.BlockSpec((d, ADA_TN), lambda j: (0, j)),
                  pl.BlockSpec((1, ADA_TN), lambda j: (0, j))],
        out_specs=pl.BlockSpec((8, ADA_TN), lambda j: (0, j)),
        out_shape=jax.ShapeDtypeStruct((8, n_cols), F32),
        compiler_params=_params("arbitrary"),
        name="adaln",
    )(c_pad, w, b.reshape(1, -1))


def _ffn_kernel(alpha, side_cast, own_cast, aliased, x_ref, sh_ref, sc_ref, gt_ref,
                wg_ref, wu_ref, wd_ref, lg_ref, lb_ref, *rest):
    rest = list(rest)
    cast_in = rest.pop(0) if side_cast else None
    if aliased:
        rest.pop(0)
    o_ref = rest.pop(0)
    casts = []
    if side_cast:
        casts.append((cast_in, rest.pop(0)))
    if own_cast:
        copies = rest[:3]
        rest = rest[3:]
        casts += list(zip((wg_ref, wu_ref, wd_ref), copies))
        wg_ref, wu_ref, wd_ref = copies
    h_ref, a_ref = rest
    j = pl.program_id(1)
    last = pl.num_programs(1) - 1

    def run_casts():
        for src, dst in casts:
            dst[...] = src[...].astype(BF16)

    def hidden():
        h = h_ref[...]
        for c in range(a_ref.shape[1] // FFN_UP_COLS):
            cols = slice(c * FFN_UP_COLS, (c + 1) * FFN_UP_COLS)
            g = jnp.dot(h, wg_ref[:, cols], preferred_element_type=F32)
            u = jnp.dot(h, wu_ref[:, cols], preferred_element_type=F32)
            a_ref[:, cols] = (g / (1.0 + jnp.exp(-g)) * u).astype(BF16)

    def down(first):
        a = a_ref[...]
        for c in range(o_ref.shape[1] // FFN_DOWN_COLS):
            cols = slice(c * FFN_DOWN_COLS, (c + 1) * FFN_DOWN_COLS)
            y = jnp.dot(a, wd_ref[:, cols], preferred_element_type=F32)
            o_ref[:, cols] = y if first else o_ref[:, cols] + y

    @pl.when(j == 0)
    def _():
        run_casts()
        h_ref[...] = (x_ref[...] * (1.0 + sc_ref[...]) + sh_ref[...]).astype(BF16)
        hidden()
        down(first=True)

    @pl.when(jnp.logical_and(j > 0, j < last))
    def _():
        run_casts()
        hidden()
        down(first=False)

    @pl.when(j == last)
    def _():
        run_casts()
        hidden()
        gate = 0.5 * (1.0 + gt_ref[...])
        for r in range(o_ref.shape[0] // ROW_CHUNK):
            rows = pl.ds(r * ROW_CHUNK, ROW_CHUNK)
            y = o_ref[rows, :] + jnp.dot(a_ref[rows, :], wd_ref[...],
                                         preferred_element_type=F32)
            z = alpha * x_ref[rows, :] + gate * y
            o_ref[rows, :] = _layer_norm_rows(z, lg_ref[...], lb_ref[...])


def _ffn_call(x, shift, scale, gate, w_gate, gate_blk0, w_up, up_blk0, w_down, ln_g, ln_b,
              alpha, *, tf, tile0, n_tiles, side_cast=None, own_cast=False, prev_out=None):
    bsz, s, d = x.shape
    f = w_down.shape[0]
    nf = f // tf
    nti = s // FFN_TM

    def token_tile(t):
        return (t + tile0) // nti, (t + tile0) % nti

    vec = pl.BlockSpec((None, 1, d), lambda t, j: (token_tile(t)[0], 0, 0))
    row = pl.BlockSpec((1, d), lambda t, j: (0, 0))
    tile = pl.BlockSpec((None, FFN_TM, d), lambda t, j: (*token_tile(t), 0))
    in_specs = [tile, vec, vec, vec,
                pl.BlockSpec((d, tf), lambda t, j: (0, gate_blk0 + j)),
                pl.BlockSpec((d, tf), lambda t, j: (0, up_blk0 + j)),
                pl.BlockSpec((tf, d), lambda t, j: (j, 0)),
                row, row]
    out_specs, out_shape = [tile], [jax.ShapeDtypeStruct((bsz, s, d), F32)]
    operands = [x, shift, scale, gate, w_gate, w_up, w_down,
                ln_g.reshape(1, d), ln_b.reshape(1, d)]
    aliases = {}
    if side_cast is not None:
        nrb, ncb = side_cast.shape[0] // CAST_BLOCK[0], side_cast.shape[1] // CAST_BLOCK[1]
        assert (nrb * CAST_BLOCK[0], ncb * CAST_BLOCK[1]) == side_cast.shape
        assert nrb * ncb <= n_tiles * nf

        def cast_index(t, j):
            blk = jnp.minimum(t * nf + j, nrb * ncb - 1)
            return blk // ncb, blk % ncb

        cast_spec = pl.BlockSpec(CAST_BLOCK, cast_index)
        in_specs.append(cast_spec)
        out_specs.append(cast_spec)
        out_shape.append(jax.ShapeDtypeStruct(side_cast.shape, BF16))
        operands.append(side_cast)
    if prev_out is not None:
        in_specs.append(pl.BlockSpec(memory_space=pl.ANY))
        operands.append(prev_out)
        aliases = {len(operands) - 1: 0}
    if own_cast:
        out_specs += [pl.BlockSpec((d, tf), lambda t, j: (0, j)),
                      pl.BlockSpec((d, tf), lambda t, j: (0, j)),
                      pl.BlockSpec((tf, d), lambda t, j: (j, 0))]
        out_shape += [jax.ShapeDtypeStruct((d, f), BF16), jax.ShapeDtypeStruct((d, f), BF16),
                      jax.ShapeDtypeStruct((f, d), BF16)]
    return pl.pallas_call(
        functools.partial(_ffn_kernel, alpha, side_cast is not None, own_cast,
                          prev_out is not None),
        grid=(n_tiles, nf),
        in_specs=in_specs,
        out_specs=out_specs,
        out_shape=out_shape,
        input_output_aliases=aliases,
        scratch_shapes=[pltpu.VMEM((FFN_TM, d), BF16), pltpu.VMEM((FFN_TM, tf), BF16)],
        compiler_params=_params("arbitrary", "arbitrary",
                                vmem_limit_bytes=FFN_VMEM_LIMIT_BYTES),
        name="ffn_ln",
    )(*operands)


def _ffn_ln(x, shift, scale, gate, w_gu, w_down, ln_g, ln_b, alpha):
    bsz, s, _ = x.shape
    nf = w_down.shape[0] // FFN_TF
    return _ffn_call(x, shift, scale, gate, w_gu, 0, w_gu, nf, w_down, ln_g, ln_b, alpha,
                     tf=FFN_TF, tile0=0, n_tiles=bsz * (s // FFN_TM))[0]


def _ffn_ln_f32_weights(x, shift, scale, gate, w_gu, w_down, ln_g, ln_b, alpha, side_cast):
    bsz, s, _ = x.shape
    n_tiles = bsz * (s // FFN_TM)
    nf_first = w_down.shape[0] // FFN_TF_F32
    x_first, w_gate_b, w_up_b, w_down_b = _ffn_call(
        x, shift, scale, gate, w_gu, 0, w_gu, nf_first, w_down, ln_g, ln_b, alpha,
        tf=FFN_TF_F32, tile0=0, n_tiles=1, own_cast=True)
    x_out, side_b = _ffn_call(
        x, shift, scale, gate, w_gate_b, 0, w_up_b, 0, w_down_b, ln_g, ln_b, alpha,
        tf=FFN_TF, tile0=1, n_tiles=n_tiles - 1, side_cast=side_cast, prev_out=x_first)
    return x_out, side_b


def _gmlp_kernel(n_cast, x_ref, sh_ref, sc_ref, wu_ref, wv_ref, lg_ref, lb_ref, ws_ref,
                 bst_ref, *rest):
    cast_in, o_ref, cast_out = rest[:n_cast], rest[n_cast], rest[n_cast + 1:]
    for src, dst in zip(cast_in, cast_out):
        dst[...] = src[...].astype(BF16)

    pi = lax.broadcasted_iota(jnp.int32, (GMLP_BLOCK, GMLP_BLOCK), 0) // CHUNK
    pj = lax.broadcasted_iota(jnp.int32, (GMLP_BLOCK, GMLP_BLOCK), 1) // CHUNK
    keep = pj <= pi
    bst = bst_ref[...]

    h = (x_ref[...] * (1.0 + sc_ref[...]) + sh_ref[...]).astype(BF16)

    def gelu_proj(w_ref):
        parts = []
        for c in range(w_ref.shape[1] // PROJ_COLS):
            cols = slice(c * PROJ_COLS, (c + 1) * PROJ_COLS)
            parts.append(jax.nn.gelu(jnp.dot(h, w_ref[:, cols], preferred_element_type=F32)))
        return jnp.concatenate(parts, axis=1)

    v = _layer_norm_rows(gelu_proj(wv_ref), lg_ref[...], lb_ref[...]).astype(BF16)
    u = gelu_proj(wu_ref)
    n_blk = x_ref.shape[0] // GMLP_BLOCK
    for g in range(GMLP_GROUPS):
        w = jnp.where(keep, ws_ref[g], 0.0).astype(BF16)
        bias = bst[:, g:g + 1]
        cols = slice(g * LANES, (g + 1) * LANES)
        v_blocks = jnp.concatenate(
            [v[blk * GMLP_BLOCK:(blk + 1) * GMLP_BLOCK, cols] for blk in range(n_blk)], axis=1)
        mixed = jnp.dot(w, v_blocks, preferred_element_type=F32) + bias
        for blk in range(n_blk):
            rows = slice(blk * GMLP_BLOCK, (blk + 1) * GMLP_BLOCK)
            o_ref[rows, cols] = (u[rows, cols]
                                 * mixed[:, blk * LANES:(blk + 1) * LANES]).astype(BF16)


def _gmlp(x, shift, scale, w_in, width, ln_g, ln_b, w_s, b_s_t, cast_weights):
    bsz, s, d = x.shape
    nt = s // MIX_TM
    vec = pl.BlockSpec((None, 1, d), lambda b, i: (b, 0, 0))
    full = lambda shape: pl.BlockSpec(shape, lambda b, i: (0,) * len(shape))
    w_cols = lambda blk: pl.BlockSpec((d, width), lambda b, i: (0, blk))
    slab_specs = []
    for w in cast_weights:
        rows = w.shape[0] // (bsz * nt)
        assert rows * bsz * nt == w.shape[0] and rows % 16 == 0, w.shape
        slab_specs.append(pl.BlockSpec((rows, w.shape[1]), lambda b, i: (b * nt + i, 0)))
    return pl.pallas_call(
        functools.partial(_gmlp_kernel, len(cast_weights)),
        grid=(bsz, nt),
        in_specs=[pl.BlockSpec((None, MIX_TM, d), lambda b, i: (b, i, 0)), vec, vec,
                  w_cols(0), w_cols(1),
                  full((1, width)), full((1, width)),
                  full(w_s.shape), full(b_s_t.shape)] + slab_specs,
        out_specs=[pl.BlockSpec((None, MIX_TM, width), lambda b, i: (b, i, 0))] + slab_specs,
        out_shape=[jax.ShapeDtypeStruct((bsz, s, width), BF16)]
        + [jax.ShapeDtypeStruct(w.shape, BF16) for w in cast_weights],
        compiler_params=_params("arbitrary", "arbitrary"),
        name="gmlp",
    )(x, shift, scale, w_in, w_in, ln_g.reshape(1, width), ln_b.reshape(1, width),
      w_s, b_s_t, *cast_weights)


def _rope(x, cos, sin_signed, first_half):
    rot = jnp.where(first_half, pltpu.roll(x, 3 * LANES // 4, 1),
                    pltpu.roll(x, LANES // 4, 1))
    return x * cos + rot * sin_signed


def _qkv_kernel(x_ref, sh_ref, sc_ref, wq_ref, wk_ref, wv_ref, cos_ref, sin_ref,
                qq_ref, k_ref, vt_ref):
    t = x_ref.shape[0]
    h = (x_ref[...] * (1.0 + sc_ref[...]) + sh_ref[...]).astype(BF16)
    cos = cos_ref[...]
    sin = sin_ref[...]
    lane = lax.broadcasted_iota(jnp.int32, (t, LANES), 1)
    first_half = (lane % DIFF_QK_DIM) < (DIFF_QK_DIM // 2)
    comp = lax.broadcasted_iota(jnp.int32, (LANES, t), 0) < DIFF_QK_DIM
    q_scale = DIFF_QK_DIM ** -0.5 * LOG2_E
    ones = jnp.ones((V_AUG_ROWS - DIFF_V_DIM, t), BF16)
    row_chunk = lax.broadcasted_iota(jnp.int32, (t, LANES), 0) // CHUNK
    chunk_onehot = jnp.where(row_chunk == lane, 1.0, 0.0).astype(BF16)
    q = jnp.dot(h, wq_ref[...], preferred_element_type=F32)
    k = jnp.dot(h, wk_ref[...], preferred_element_type=F32)
    v = jnp.dot(h, wv_ref[...], preferred_element_type=F32)
    for hd in range(DIFF_HEADS):
        cols = slice(hd * LANES, (hd + 1) * LANES)
        k_ref[hd, :, :LANES] = _rope(k[:, cols], cos, sin, first_half).astype(BF16)
        k_ref[hd, :, LANES:] = chunk_onehot
        qt = (_rope(q[:, cols], cos, sin, first_half) * q_scale).T
        qq_ref[hd, 0, :, :t] = jnp.where(comp, qt, 0.0).astype(BF16)
        qq_ref[hd, 0, :, t:] = jnp.where(comp, 0.0, qt).astype(BF16)
        vt_ref[hd, 0, :DIFF_V_DIM, :] = v[:, cols].T.astype(BF16)
        vt_ref[hd, 0, DIFF_V_DIM:, :] = ones


def _qkv_rope(x, shift, scale, w_in, width, cos, sin_signed):
    bsz, s, d = x.shape
    t = ATT_T
    nt = s // t
    vec = pl.BlockSpec((None, 1, d), lambda b, i: (b, 0, 0))
    w_cols = lambda blk: pl.BlockSpec((d, width), lambda b, i: (0, blk))
    tab = pl.BlockSpec((t, LANES), lambda b, i: (i, 0))
    return pl.pallas_call(
        _qkv_kernel,
        grid=(bsz, nt),
        in_specs=[pl.BlockSpec((None, t, d), lambda b, i: (b, i, 0)), vec, vec,
                  w_cols(2), w_cols(3), w_cols(4), tab, tab],
        out_specs=[
            pl.BlockSpec((None, DIFF_HEADS, 1, LANES, 2 * t), lambda b, i: (b, 0, i, 0, 0)),
            pl.BlockSpec((None, DIFF_HEADS, t, 2 * LANES), lambda b, i: (b, 0, i, 0)),
            pl.BlockSpec((None, DIFF_HEADS, 1, V_AUG_ROWS, t), lambda b, i: (b, 0, i, 0, 0)),
        ],
        out_shape=[
            jax.ShapeDtypeStruct((bsz, DIFF_HEADS, nt, LANES, 2 * t), BF16),
            jax.ShapeDtypeStruct((bsz, DIFF_HEADS, s, 2 * LANES), BF16),
            jax.ShapeDtypeStruct((bsz, DIFF_HEADS, nt, V_AUG_ROWS, t), BF16),
        ],
        compiler_params=_params("arbitrary", "arbitrary"),
        name="qkv_rope",
    )(x, shift, scale, w_in, w_in, w_in, cos, sin_signed)


def _attn_kernel(lam_init, n_steps, sched_ref, qq_ref, k_ref, vt_ref, bias_ref,
                 lq1_ref, lk1_ref, lq2_ref, lk2_ref, sg_ref, c_ref, wada_ref, bada_ref,
                 o_ref, mod_ref,
                 m_ref, acc_ref, fin_ref, s0_ref, s1_ref, smax0_ref, smax1_ref):
    t = ATT_T
    mod_ref[...] = _adaln_block(c_ref, wada_ref, bada_ref)

    m_ref[...] = jnp.full_like(m_ref, NEG_INF)
    acc_ref[...] = jnp.zeros_like(acc_ref)
    buffers = ((s0_ref, smax0_ref), (s1_ref, smax1_ref))

    def produce(n, s_ref, smax_ref):
        qi, kj, dg = sched_ref[0, n], sched_ref[1, n], sched_ref[2, n]
        rhs = jnp.concatenate([qq_ref[qi], bias_ref[dg]], axis=0)
        kblk = k_ref[pl.ds(pl.multiple_of(kj * t, t), t), :]
        s = jnp.dot(kblk, rhs, preferred_element_type=F32)
        s_ref[...] = s
        smax_ref[...] = jnp.max(s, axis=0, keepdims=True)

    def consume(n, s_ref, smax_ref):
        qi, kj = sched_ref[0, n], sched_ref[1, n]
        s = s_ref[...]
        m_prev = jnp.where(kj == 0, NEG_INF, m_ref[...])
        m_new = jnp.maximum(m_prev, smax_ref[...])
        alpha = jnp.exp2(m_prev - m_new)
        p = jnp.exp2(s - m_new).astype(BF16)
        pv = jnp.dot(vt_ref[kj], p, preferred_element_type=F32)
        acc = alpha * acc_ref[...] + pv
        acc_ref[...] = acc
        fin_ref[qi] = acc
        m_ref[...] = m_new

    produce(0, *buffers[0])
    produce(1, *buffers[1])

    def trip(g, carry):
        for u in range(ATT_UNROLL):
            n = g * ATT_UNROLL + u
            consume(n, *buffers[u % 2])
            produce(n + 2, *buffers[u % 2])
        return carry

    lax.fori_loop(0, n_steps // ATT_UNROLL, trip, 0)

    lam = (jnp.exp(jnp.sum(lq1_ref[...] * lk1_ref[...], keepdims=True))
           - jnp.exp(jnp.sum(lq2_ref[...] * lk2_ref[...], keepdims=True))
           + lam_init)
    sub_g = sg_ref[...] * (1.0 - lam_init)

    def finalize(qi, carry):
        acc = fin_ref[qi]
        o = acc[:DIFF_V_DIM, :] / acc[DIFF_V_DIM:DIFF_V_DIM + 1, :]
        o = o[:, :t] - lam * o[:, t:]
        ms = jnp.mean(o * o, axis=0, keepdims=True)
        o = o * lax.rsqrt(ms + LN_EPS) * sub_g
        o_ref[pl.ds(pl.multiple_of(qi * t, t), t), :] = o.T.astype(BF16)
        return carry

    lax.fori_loop(0, fin_ref.shape[0], finalize, 0)


def _attn_schedule(nt):
    steps = [(qi, kj, int(kj == qi)) for qi in range(nt) for kj in range(qi + 1)]
    n_steps = len(steps)
    steps += [(0, 0, 1)] * 2
    return jnp.asarray(steps, jnp.int32).T, n_steps


def _attn_mask_bias(t):
    key_chunk = np.arange(LANES)[:, None]
    q_chunk = (np.arange(2 * t)[None, :] % t) // CHUNK
    hidden = (key_chunk < t // CHUNK) & (key_chunk > q_chunk)
    diag = np.where(hidden, NEG_INF, 0.0).astype(np.float32)
    return jnp.asarray(np.stack([np.zeros_like(diag), diag]), dtype=BF16)


def _diff_attn(qq, k, vt, lq1, lk1, lq2, lk2, sub_g, lam_init, c_pad, w_ada, b_ada, ada_col0):
    bsz, heads, nt, _, _ = qq.shape
    t = ATT_T
    s = nt * t
    sched, n_steps = _attn_schedule(nt)
    assert n_steps % ATT_UNROLL == 0 and ATT_UNROLL % 2 == 0
    n_tail = w_ada.shape[1] - ada_col0
    ada_tn = n_tail // (bsz * heads)
    assert ada_tn * bsz * heads == n_tail and ada_tn % LANES == 0 and ada_col0 % ada_tn == 0
    ada_blk0 = ada_col0 // ada_tn
    lam_spec = pl.BlockSpec((1, DIFF_QK_DIM), lambda b, h, sc: (0, 0))
    grid_spec = pltpu.PrefetchScalarGridSpec(
        num_scalar_prefetch=1,
        grid=(bsz, heads),
        in_specs=[
            pl.BlockSpec((None, None, nt, LANES, 2 * t), lambda b, h, sc: (b, h, 0, 0, 0)),
            pl.BlockSpec((None, None, s, 2 * LANES), lambda b, h, sc: (b, h, 0, 0)),
            pl.BlockSpec((None, None, nt, V_AUG_ROWS, t), lambda b, h, sc: (b, h, 0, 0, 0)),
            pl.BlockSpec((2, LANES, 2 * t), lambda b, h, sc: (0, 0, 0)),
            lam_spec, lam_spec, lam_spec, lam_spec,
            pl.BlockSpec((DIFF_V_DIM, 1), lambda b, h, sc: (0, 0)),
            pl.BlockSpec(c_pad.shape, lambda b, h, sc: (0, 0)),
            pl.BlockSpec((w_ada.shape[0], ada_tn),
                         lambda b, h, sc: (0, ada_blk0 + b * heads + h)),
            pl.BlockSpec((1, ada_tn), lambda b, h, sc: (0, ada_blk0 + b * heads + h)),
        ],
        out_specs=[pl.BlockSpec((None, s, LANES), lambda b, h, sc: (b, 0, h)),
                   pl.BlockSpec((8, ada_tn), lambda b, h, sc: (0, b * heads + h))],
        scratch_shapes=[pltpu.VMEM((1, 2 * t), F32),
                        pltpu.VMEM((V_AUG_ROWS, 2 * t), F32),
                        pltpu.VMEM((nt, V_AUG_ROWS, 2 * t), F32),
                        pltpu.VMEM((t, 2 * t), F32), pltpu.VMEM((t, 2 * t), F32),
                        pltpu.VMEM((1, 2 * t), F32), pltpu.VMEM((1, 2 * t), F32)],
    )
    return pl.pallas_call(
        functools.partial(_attn_kernel, lam_init, n_steps),
        grid_spec=grid_spec,
        out_shape=[jax.ShapeDtypeStruct((bsz, s, heads * DIFF_V_DIM), BF16),
                   jax.ShapeDtypeStruct((8, n_tail), F32)],
        compiler_params=_params("arbitrary", "arbitrary"),
        name="diff_attn",
    )(sched, qq, k, vt, _attn_mask_bias(t), lq1.reshape(1, -1), lk1.reshape(1, -1),
      lq2.reshape(1, -1), lk2.reshape(1, -1), sub_g.reshape(-1, 1),
      c_pad, w_ada, b_ada.reshape(1, -1))


def _out_kernel(alpha, x_ref, gt_ref, a_ref, b_ref, wa_ref, wb_ref, lg_ref, lb_ref,
                o_ref):
    gate = 1.0 + gt_ref[...]
    for c in range(x_ref.shape[0] // ROW_CHUNK):
        rows = pl.ds(c * ROW_CHUNK, ROW_CHUNK)
        y = jnp.dot(a_ref[rows, :], wa_ref[...], preferred_element_type=F32)
        y = y + jnp.dot(b_ref[rows, :], wb_ref[...], preferred_element_type=F32)
        z = alpha * x_ref[rows, :] + gate * y
        o_ref[rows, :] = _layer_norm_rows(z, lg_ref[...], lb_ref[...])


def _out_ln(x, gate, out_a, out_b, w_out, ln_g, ln_b, alpha):
    bsz, s, d = x.shape
    half = out_a.shape[-1]
    tile = pl.BlockSpec((None, MIX_TM, d), lambda b, i: (b, i, 0))
    act = pl.BlockSpec((None, MIX_TM, half), lambda b, i: (b, i, 0))
    row = pl.BlockSpec((1, d), lambda b, i: (0, 0))
    return pl.pallas_call(
        functools.partial(_out_kernel, alpha),
        grid=(bsz, s // MIX_TM),
        in_specs=[tile, pl.BlockSpec((None, 1, d), lambda b, i: (b, 0, 0)), act, act,
                  pl.BlockSpec((half, d), lambda b, i: (0, 0)),
                  pl.BlockSpec((half, d), lambda b, i: (1, 0)),
                  row, row],
        out_specs=tile,
        out_shape=jax.ShapeDtypeStruct((bsz, s, d), F32),
        compiler_params=_params("arbitrary", "arbitrary"),
        name="out_ln",
    )(x, gate, out_a, out_b, w_out, w_out, ln_g.reshape(1, d), ln_b.reshape(1, d))


def _rope_tables(s):
    half = DIFF_QK_DIM // 2
    pos = np.arange(s, dtype=np.float64)
    inv_freq = ROPE_THETA ** (-np.arange(0, DIFF_QK_DIM, 2, dtype=np.float64) / DIFF_QK_DIM)
    ang = pos[:, None] * inv_freq[None, :]
    cos, sin = np.cos(ang), np.sin(ang)
    cos = np.concatenate([cos, cos, cos, cos], axis=-1)
    sin_signed = np.concatenate([-sin, sin, -sin, sin], axis=-1)
    assert cos.shape[-1] == 4 * half == LANES
    return jnp.asarray(cos, dtype=F32), jnp.asarray(sin_signed, dtype=F32)


def kernel(x, c, w_ada, b_ada, ffn1_w_gu, ffn1_w_down, ln1_g, ln1_b, w_in, gmlp_ln_g, gmlp_ln_b, gmlp_w_s, gmlp_b_s, lambda_q1, lambda_k1, lambda_q2, lambda_k2, diff_subln_g, w_out, ln2_g, ln2_b, ffn2_w_gu, ffn2_w_down, ln3_g, ln3_b):
    bsz, s, d = x.shape
    depth = w_ada.shape[0]
    alpha = (2 * depth) ** 0.25
    gw = gmlp_ln_g.shape[-1]
    qk = DIFF_HEADS * 2 * DIFF_QK_DIM
    cos, sin_signed = _rope_tables(s)
    c_pad = jnp.zeros((8, d), F32).at[:bsz].set(c)

    assert qk == gw and w_in.shape[-1] == 5 * gw
    n_early = 5

    for l in range(depth):
        mod_early = _adaln(c_pad, w_ada[l], b_ada[l], n_early * d)[:bsz]
        s1, sc1, g1, s2, sc2 = [m[:, None, :] for m in jnp.split(mod_early, n_early, axis=-1)]

        x, w_in_b = _ffn_ln_f32_weights(x, s1, sc1, g1, ffn1_w_gu[l], ffn1_w_down[l],
                                        ln1_g[l], ln1_b[l], alpha, side_cast=w_in[l])
        out_a, w_out_b, w_gu2_b, w_down2_b = _gmlp(
            x, s2, sc2, w_in_b, gw, gmlp_ln_g[l], gmlp_ln_b[l],
            gmlp_w_s[l], jnp.transpose(gmlp_b_s[l]),
            cast_weights=(w_out[l], ffn2_w_gu[l], ffn2_w_down[l]))
        qq, k, vt = _qkv_rope(x, s2, sc2, w_in_b, gw, cos, sin_signed)
        lam_init = 0.8 - 0.6 * math.exp(-0.3 * l)
        out_b, mod_late = _diff_attn(qq, k, vt, lambda_q1[l], lambda_k1[l], lambda_q2[l],
                                     lambda_k2[l], diff_subln_g[l], lam_init,
                                     c_pad, w_ada[l], b_ada[l], n_early * d)
        g2, s3, sc3, g3 = [m[:, None, :]
                           for m in jnp.split(mod_late[:bsz], N_MOD - n_early, axis=-1)]
        x = _out_ln(x, g2, out_a, out_b, w_out_b, ln2_g[l], ln2_b[l], alpha)

        x = _ffn_ln(x, s3, sc3, g3, w_gu2_b, w_down2_b, ln3_g[l], ln3_b[l], alpha)
    return x
```

```python
import functools
import math

import jax
import jax.numpy as jnp
import numpy as np
from jax import lax
from jax.experimental import pallas as pl
from jax.experimental.pallas import tpu as pltpu

F32 = jnp.float32
BF16 = jnp.bfloat16

CHUNK = 64
GMLP_BLOCK = 128
GMLP_GROUPS = 8
DIFF_HEADS = 8
DIFF_V_DIM = 128
DIFF_QK_DIM = 64
N_MOD = 9
ROPE_THETA = 10000.0
LN_EPS = 1e-5
NEG_INF = -1e30
LOG2_E = 1.4426950408889634
V_AUG_ROWS = DIFF_V_DIM + 16

LANES = 128
VMEM_LIMIT_BYTES = 56 * 1024 * 1024
FFN_VMEM_LIMIT_BYTES = 61 * 1024 * 1024

ADA_TN = 1024
FFN_TM = 1024
FFN_TF = 512
FFN_TF_F32 = 256
FFN_UP_COLS = 256
FFN_DOWN_COLS = 256
CAST_BLOCK = (128, 512)
MIX_TM = 512
ROW_CHUNK = 256
PROJ_COLS = 256
ATT_T = 512
ATT_UNROLL = 34


def _params(*sem, vmem_limit_bytes=VMEM_LIMIT_BYTES):
    return pltpu.CompilerParams(dimension_semantics=sem,
                                vmem_limit_bytes=vmem_limit_bytes)


def _layer_norm_rows(z, g, b):
    mu = jnp.mean(z, axis=-1, keepdims=True)
    zc = z - mu
    var = jnp.mean(zc * zc, axis=-1, keepdims=True)
    return zc * lax.rsqrt(var + LN_EPS) * g + b


def _adaln_block(c_ref, w_ref, b_ref):
    c = c_ref[...]
    ca = (c / (1.0 + jnp.exp(-c))).astype(BF16)
    return jnp.dot(ca, w_ref[...].astype(BF16), preferred_element_type=F32) + b_ref[...]


def _adaln_kernel(c_ref, w_ref, b_ref, o_ref):
    o_ref[...] = _adaln_block(c_ref, w_ref, b_ref)


def _adaln(c_pad, w, b, n_cols):
    d = w.shape[0]
    return pl.pallas_call(
        _adaln_kernel,
        grid=(n_cols // ADA_TN,),
        in_specs=[pl.BlockSpec((8, d), lambda j: (0, 0)),
                  pl.BlockSpec((d, ADA_TN), lambda j: (0, j)),
                  pl.BlockSpec((1, ADA_TN), lambda j: (0, j))],
        out_specs=pl.BlockSpec((8, ADA_TN), lambda j: (0, j)),
        out_shape=jax.ShapeDtypeStruct((8, n_cols), F32),
        compiler_params=_params("arbitrary"),
        name="adaln",
    )(c_pad, w, b.reshape(1, -1))


def _ffn_kernel(alpha, side_cast, own_cast, aliased, x_ref, sh_ref, sc_ref, gt_ref,
                wg_ref, wu_ref, wd_ref, lg_ref, lb_ref, *rest):
    rest = list(rest)
    cast_in = rest.pop(0) if side_cast else None
    if aliased:
        rest.pop(0)
    o_ref = rest.pop(0)
    casts = []
    if side_cast:
        casts.append((cast_in, rest.pop(0)))
    if own_cast:
        copies = rest[:3]
        rest = rest[3:]
        casts += list(zip((wg_ref, wu_ref, wd_ref), copies))
        wg_ref, wu_ref, wd_ref = copies
    h_ref, a_ref = rest
    j = pl.program_id(1)
    last = pl.num_programs(1) - 1

    def run_casts():
        for src, dst in casts:
            dst[...] = src[...].astype(BF16)

    def hidden():
        h = h_ref[...]
        for c in range(a_ref.shape[1] // FFN_UP_COLS):
            cols = slice(c * FFN_UP_COLS, (c + 1) * FFN_UP_COLS)
            g = jnp.dot(h, wg_ref[:, cols], preferred_element_type=F32)
            u = jnp.dot(h, wu_ref[:, cols], preferred_element_type=F32)
            a_ref[:, cols] = (g / (1.0 + jnp.exp(-g)) * u).astype(BF16)

    def down(first):
        a = a_ref[...]
        for c in range(o_ref.shape[1] // FFN_DOWN_COLS):
            cols = slice(c * FFN_DOWN_COLS, (c + 1) * FFN_DOWN_COLS)
            y = jnp.dot(a, wd_ref[:, cols], preferred_element_type=F32)
            o_ref[:, cols] = y if first else o_ref[:, cols] + y

    @pl.when(j == 0)
    def _():
        run_casts()
        h_ref[...] = (x_ref[...] * (1.0 + sc_ref[...]) + sh_ref[...]).astype(BF16)
        hidden()
        down(first=True)

    @pl.when(jnp.logical_and(j > 0, j < last))
    def _():
        run_casts()
        hidden()
        down(first=False)

    @pl.when(j == last)
    def _():
        run_casts()
        hidden()
        gate = 0.5 * (1.0 + gt_ref[...])
        for r in range(o_ref.shape[0] // ROW_CHUNK):
            rows = pl.ds(r * ROW_CHUNK, ROW_CHUNK)
            y = o_ref[rows, :] + jnp.dot(a_ref[rows, :], wd_ref[...],
                                         preferred_element_type=F32)
            z = alpha * x_ref[rows, :] + gate * y
            o_ref[rows, :] = _layer_norm_rows(z, lg_ref[...], lb_ref[...])


def _ffn_call(x, shift, scale, gate, w_gate, gate_blk0, w_up, up_blk0, w_down, ln_g, ln_b,
              alpha, *, tf, tile0, n_tiles, side_cast=None, own_cast=False, prev_out=None):
    bsz, s, d = x.shape
    f = w_down.shape[0]
    nf = f // tf
    nti = s // FFN_TM

    def token_tile(t):
        return (t + tile0) // nti, (t + tile0) % nti

    vec = pl.BlockSpec((None, 1, d), lambda t, j: (token_tile(t)[0], 0, 0))
    row = pl.BlockSpec((1, d), lambda t, j: (0, 0))
    tile = pl.BlockSpec((None, FFN_TM, d), lambda t, j: (*token_tile(t), 0))
    in_specs = [tile, vec, vec, vec,
                pl.BlockSpec((d, tf), lambda t, j: (0, gate_blk0 + j)),
                pl.BlockSpec((d, tf), lambda t, j: (0, up_blk0 + j)),
                pl.BlockSpec((tf, d), lambda t, j: (j, 0)),
                row, row]
    out_specs, out_shape = [tile], [jax.ShapeDtypeStruct((bsz, s, d), F32)]
    operands = [x, shift, scale, gate, w_gate, w_up, w_down,
                ln_g.reshape(1, d), ln_b.reshape(1, d)]
    aliases = {}
    if side_cast is not None:
        nrb, ncb = side_cast.shape[0] // CAST_BLOCK[0], side_cast.shape[1] // CAST_BLOCK[1]
        assert (nrb * CAST_BLOCK[0], ncb * CAST_BLOCK[1]) == side_cast.shape
        assert nrb * ncb <= n_tiles * nf

        def cast_index(t, j):
            blk = jnp.minimum(t * nf + j, nrb * ncb - 1)
            return blk // ncb, blk % ncb

        cast_spec = pl.BlockSpec(CAST_BLOCK, cast_index)
        in_specs.append(cast_spec)
        out_specs.append(cast_spec)
        out_shape.append(jax.ShapeDtypeStruct(side_cast.shape, BF16))
        operands.append(side_cast)
    if prev_out is not None:
        in_specs.append(pl.BlockSpec(memory_space=pl.ANY))
        operands.append(prev_out)
        aliases = {len(operands) - 1: 0}
    if own_cast:
        out_specs += [pl.BlockSpec((d, tf), lambda t, j: (0, j)),
                      pl.BlockSpec((d, tf), lambda t, j: (0, j)),
                      pl.BlockSpec((tf, d), lambda t, j: (j, 0))]
        out_shape += [jax.ShapeDtypeStruct((d, f), BF16), jax.ShapeDtypeStruct((d, f), BF16),
                      jax.ShapeDtypeStruct((f, d), BF16)]
    return pl.pallas_call(
        functools.partial(_ffn_kernel, alpha, side_cast is not None, own_cast,
                          prev_out is not None),
        grid=(n_tiles, nf),
        in_specs=in_specs,
        out_specs=out_specs,
        out_shape=out_shape,
        input_output_aliases=aliases,
        scratch_shapes=[pltpu.VMEM((FFN_TM, d), BF16), pltpu.VMEM((FFN_TM, tf), BF16)],
        compiler_params=_params("arbitrary", "arbitrary",
                                vmem_limit_bytes=FFN_VMEM_LIMIT_BYTES),
        name="ffn_ln",
    )(*operands)


def _ffn_ln(x, shift, scale, gate, w_gu, w_down, ln_g, ln_b, alpha):
    bsz, s, _ = x.shape
    nf = w_down.shape[0] // FFN_TF
    return _ffn_call(x, shift, scale, gate, w_gu, 0, w_gu, nf, w_down, ln_g, ln_b, alpha,
                     tf=FFN_TF, tile0=0, n_tiles=bsz * (s // FFN_TM))[0]


def _ffn_ln_f32_weights(x, shift, scale, gate, w_gu, w_down, ln_g, ln_b, alpha, side_cast):
    bsz, s, _ = x.shape
    n_tiles = bsz * (s // FFN_TM)
    nf_first = w_down.shape[0] // FFN_TF_F32
    x_first, w_gate_b, w_up_b, w_down_b = _ffn_call(
        x, shift, scale, gate, w_gu, 0, w_gu, nf_first, w_down, ln_g, ln_b, alpha,
        tf=FFN_TF_F32, tile0=0, n_tiles=1, own_cast=True)
    x_out, side_b = _ffn_call(
        x, shift, scale, gate, w_gate_b, 0, w_up_b, 0, w_down_b, ln_g, ln_b, alpha,
        tf=FFN_TF, tile0=1, n_tiles=n_tiles - 1, side_cast=side_cast, prev_out=x_first)
    return x_out, side_b


def _gmlp_kernel(n_cast, x_ref, sh_ref, sc_ref, wu_ref, wv_ref, lg_ref, lb_ref, ws_ref,
                 bst_ref, *rest):
    cast_in, o_ref, cast_out = rest[:n_cast], rest[n_cast], rest[n_cast + 1:]
    for src, dst in zip(cast_in, cast_out):
        dst[...] = src[...].astype(BF16)

    pi = lax.broadcasted_iota(jnp.int32, (GMLP_BLOCK, GMLP_BLOCK), 0) // CHUNK
    pj = lax.broadcasted_iota(jnp.int32, (GMLP_BLOCK, GMLP_BLOCK), 1) // CHUNK
    keep = pj <= pi
    bst = bst_ref[...]

    h = (x_ref[...] * (1.0 + sc_ref[...]) + sh_ref[...]).astype(BF16)

    def gelu_proj(w_ref):
        parts = []
        for c in range(w_ref.shape[1] // PROJ_COLS):
            cols = slice(c * PROJ_COLS, (c + 1) * PROJ_COLS)
            parts.append(jax.nn.gelu(jnp.dot(h, w_ref[:, cols], preferred_element_type=F32)))
        return jnp.concatenate(parts, axis=1)

    v = _layer_norm_rows(gelu_proj(wv_ref), lg_ref[...], lb_ref[...]).astype(BF16)
    u = gelu_proj(wu_ref)
    n_blk = x_ref.shape[0] // GMLP_BLOCK
    for g in range(GMLP_GROUPS):
        w = jnp.where(keep, ws_ref[g], 0.0).astype(BF16)
        bias = bst[:, g:g + 1]
        cols = slice(g * LANES, (g + 1) * LANES)
        v_blocks = jnp.concatenate(
            [v[blk * GMLP_BLOCK:(blk + 1) * GMLP_BLOCK, cols] for blk in range(n_blk)], axis=1)
        mixed = jnp.dot(w, v_blocks, preferred_element_type=F32) + bias
        for blk in range(n_blk):
            rows = slice(blk * GMLP_BLOCK, (blk + 1) * GMLP_BLOCK)
            o_ref[rows, cols] = (u[rows, cols]
                                 * mixed[:, blk * LANES:(blk + 1) * LANES]).astype(BF16)


def _gmlp(x, shift, scale, w_in, width, ln_g, ln_b, w_s, b_s_t, cast_weights):
    bsz, s, d = x.shape
    nt = s // MIX_TM
    vec = pl.BlockSpec((None, 1, d), lambda b, i: (b, 0, 0))
    full = lambda shape: pl.BlockSpec(shape, lambda b, i: (0,) * len(shape))
    w_cols = lambda blk: pl.BlockSpec((d, width), lambda b, i: (0, blk))
    slab_specs = []
    for w in cast_weights:
        rows = w.shape[0] // (bsz * nt)
        assert rows * bsz * nt == w.shape[0] and rows % 16 == 0, w.shape
        slab_specs.append(pl.BlockSpec((rows, w.shape[1]), lambda b, i: (b * nt + i, 0)))
    return pl.pallas_call(
        functools.partial(_gmlp_kernel, len(cast_weights)),
        grid=(bsz, nt),
        in_specs=[pl.BlockSpec((None, MIX_TM, d), lambda b, i: (b, i, 0)), vec, vec,
                  w_cols(0), w_cols(1),
                  full((1, width)), full((1, width)),
                  full(w_s.shape), full(b_s_t.shape)] + slab_specs,
        out_specs=[pl.BlockSpec((None, MIX_TM, width), lambda b, i: (b, i, 0))] + slab_specs,
        out_shape=[jax.ShapeDtypeStruct((bsz, s, width), BF16)]
        + [jax.ShapeDtypeStruct(w.shape, BF16) for w in cast_weights],
        compiler_params=_params("arbitrary", "arbitrary"),
        name="gmlp",
    )(x, shift, scale, w_in, w_in, ln_g.reshape(1, width), ln_b.reshape(1, width),
      w_s, b_s_t, *cast_weights)


def _rope(x, cos, sin_signed, first_half):
    rot = jnp.where(first_half, pltpu.roll(x, 3 * LANES // 4, 1),
                    pltpu.roll(x, LANES // 4, 1))
    return x * cos + rot * sin_signed


def _qkv_kernel(x_ref, sh_ref, sc_ref, wq_ref, wk_ref, wv_ref, cos_ref, sin_ref,
                qq_ref, k_ref, vt_ref):
    t = x_ref.shape[0]
    h = (x_ref[...] * (1.0 + sc_ref[...]) + sh_ref[...]).astype(BF16)
    cos = cos_ref[...]
    sin = sin_ref[...]
    lane = lax.broadcasted_iota(jnp.int32, (t, LANES), 1)
    first_half = (lane % DIFF_QK_DIM) < (DIFF_QK_DIM // 2)
    comp = lax.broadcasted_iota(jnp.int32, (LANES, t), 0) < DIFF_QK_DIM
    q_scale = DIFF_QK_DIM ** -0.5 * LOG2_E
    ones = jnp.ones((V_AUG_ROWS - DIFF_V_DIM, t), BF16)
    row_chunk = lax.broadcasted_iota(jnp.int32, (t, LANES), 0) // CHUNK
    chunk_onehot = jnp.where(row_chunk == lane, 1.0, 0.0).astype(BF16)
    q = jnp.dot(h, wq_ref[...], preferred_element_type=F32)
    k = jnp.dot(h, wk_ref[...], preferred_element_type=F32)
    v = jnp.dot(h, wv_ref[...], preferred_element_type=F32)
    for hd in range(DIFF_HEADS):
        cols = slice(hd * LANES, (hd + 1) * LANES)
        k_ref[hd, :, :LANES] = _rope(k[:, cols], cos, sin, first_half).astype(BF16)
        k_ref[hd, :, LANES:] = chunk_onehot
        qt = (_rope(q[:, cols], cos, sin, first_half) * q_scale).T
        qq_ref[hd, 0, :, :t] = jnp.where(comp, qt, 0.0).astype(BF16)
        qq_ref[hd, 0, :, t:] = jnp.where(comp, 0.0, qt).astype(BF16)
        vt_ref[hd, 0, :DIFF_V_DIM, :] = v[:, cols].T.astype(BF16)
        vt_ref[hd, 0, DIFF_V_DIM:, :] = ones


def _qkv_rope(x, shift, scale, w_in, width, cos, sin_signed):
    bsz, s, d = x.shape
    t = ATT_T
    nt = s // t
    vec = pl.BlockSpec((None, 1, d), lambda b, i: (b, 0, 0))
    w_cols = lambda blk: pl.BlockSpec((d, width), lambda b, i: (0, blk))
    tab = pl.BlockSpec((t, LANES), lambda b, i: (i, 0))
    return pl.pallas_call(
        _qkv_kernel,
        grid=(bsz, nt),
        in_specs=[pl.BlockSpec((None, t, d), lambda b, i: (b, i, 0)), vec, vec,
                  w_cols(2), w_cols(3), w_cols(4), tab, tab],
        out_specs=[
            pl.BlockSpec((None, DIFF_HEADS, 1, LANES, 2 * t), lambda b, i: (b, 0, i, 0, 0)),
            pl.BlockSpec((None, DIFF_HEADS, t, 2 * LANES), lambda b, i: (b, 0, i, 0)),
            pl.BlockSpec((None, DIFF_HEADS, 1, V_AUG_ROWS, t), lambda b, i: (b, 0, i, 0, 0)),
        ],
        out_shape=[
            jax.ShapeDtypeStruct((bsz, DIFF_HEADS, nt, LANES, 2 * t), BF16),
            jax.ShapeDtypeStruct((bsz, DIFF_HEADS, s, 2 * LANES), BF16),
            jax.ShapeDtypeStruct((bsz, DIFF_HEADS, nt, V_AUG_ROWS, t), BF16),
        ],
        compiler_params=_params("arbitrary", "arbitrary"),
        name="qkv_rope",
    )(x, shift, scale, w_in, w_in, w_in, cos, sin_signed)


def _attn_kernel(lam_init, n_steps, sched_ref, qq_ref, k_ref, vt_ref, bias_ref,
                 lq1_ref, lk1_ref, lq2_ref, lk2_ref, sg_ref, c_ref, wada_ref, bada_ref,
                 o_ref, mod_ref,
                 m_ref, acc_ref, fin_ref, s0_ref, s1_ref, smax0_ref, smax1_ref):
    t = ATT_T
    mod_ref[...] = _adaln_block(c_ref, wada_ref, bada_ref)

    m_ref[...] = jnp.full_like(m_ref, NEG_INF)
    acc_ref[...] = jnp.zeros_like(acc_ref)
    buffers = ((s0_ref, smax0_ref), (s1_ref, smax1_ref))

    def produce(n, s_ref, smax_ref):
        qi, kj, dg = sched_ref[0, n], sched_ref[1, n], sched_ref[2, n]
        rhs = jnp.concatenate([qq_ref[qi], bias_ref[dg]], axis=0)
        kblk = k_ref[pl.ds(pl.multiple_of(kj * t, t), t), :]
        s = jnp.dot(kblk, rhs, preferred_element_type=F32)
        s_ref[...] = s
        smax_ref[...] = jnp.max(s, axis=0, keepdims=True)

    def consume(n, s_ref, smax_ref):
        qi, kj = sched_ref[0, n], sched_ref[1, n]
        s = s_ref[...]
        m_prev = jnp.where(kj == 0, NEG_INF, m_ref[...])
        m_new = jnp.maximum(m_prev, smax_ref[...])
        alpha = jnp.exp2(m_prev - m_new)
        p = jnp.exp2(s - m_new).astype(BF16)
        pv = jnp.dot(vt_ref[kj], p, preferred_element_type=F32)
        acc = alpha * acc_ref[...] + pv
        acc_ref[...] = acc
        fin_ref[qi] = acc
        m_ref[...] = m_new

    produce(0, *buffers[0])
    produce(1, *buffers[1])

    def trip(g, carry):
        for u in range(ATT_UNROLL):
            n = g * ATT_UNROLL + u
            consume(n, *buffers[u % 2])
            produce(n + 2, *buffers[u % 2])
        return carry

    lax.fori_loop(0, n_steps // ATT_UNROLL, trip, 0)

    lam = (jnp.exp(jnp.sum(lq1_ref[...] * lk1_ref[...], keepdims=True))
           - jnp.exp(jnp.sum(lq2_ref[...] * lk2_ref[...], keepdims=True))
           + lam_init)
    sub_g = sg_ref[...] * (1.0 - lam_init)

    for qi in range(fin_ref.shape[0]):
        acc = fin_ref[qi]
        o = acc[:DIFF_V_DIM, :] / acc[DIFF_V_DIM:DIFF_V_DIM + 1, :]
        o = o[:, :t] - lam * o[:, t:]
        ms = jnp.mean(o * o, axis=0, keepdims=True)
        o = o * lax.rsqrt(ms + LN_EPS) * sub_g
        o_ref[qi * t:(qi + 1) * t, :] = o.T.astype(BF16)


def _attn_schedule(nt):
    steps = [(qi, kj, int(kj == qi)) for qi in range(nt) for kj in range(qi + 1)]
    n_steps = len(steps)
    steps += [(0, 0, 1)] * 2
    return jnp.asarray(steps, jnp.int32).T, n_steps


def _attn_mask_bias(t):
    key_chunk = np.arange(LANES)[:, None]
    q_chunk = (np.arange(2 * t)[None, :] % t) // CHUNK
    hidden = (key_chunk < t // CHUNK) & (key_chunk > q_chunk)
    diag = np.where(hidden, NEG_INF, 0.0).astype(np.float32)
    return jnp.asarray(np.stack([np.zeros_like(diag), diag]), dtype=BF16)


def _diff_attn(qq, k, vt, lq1, lk1, lq2, lk2, sub_g, lam_init, c_pad, w_ada, b_ada, ada_col0):
    bsz, heads, nt, _, _ = qq.shape
    t = ATT_T
    s = nt * t
    sched, n_steps = _attn_schedule(nt)
    assert n_steps % ATT_UNROLL == 0 and ATT_UNROLL % 2 == 0
    n_tail = w_ada.shape[1] - ada_col0
    ada_tn = n_tail // (bsz * heads)
    assert ada_tn * bsz * heads == n_tail and ada_tn % LANES == 0 and ada_col0 % ada_tn == 0
    ada_blk0 = ada_col0 // ada_tn
    lam_spec = pl.BlockSpec((1, DIFF_QK_DIM), lambda b, h, sc: (0, 0))
    grid_spec = pltpu.PrefetchScalarGridSpec(
        num_scalar_prefetch=1,
        grid=(bsz, heads),
        in_specs=[
            pl.BlockSpec((None, None, nt, LANES, 2 * t), lambda b, h, sc: (b, h, 0, 0, 0)),
            pl.BlockSpec((None, None, s, 2 * LANES), lambda b, h, sc: (b, h, 0, 0)),
            pl.BlockSpec((None, None, nt, V_AUG_ROWS, t), lambda b, h, sc: (b, h, 0, 0, 0)),
            pl.BlockSpec((2, LANES, 2 * t), lambda b, h, sc: (0, 0, 0)),
            lam_spec, lam_spec, lam_spec, lam_spec,
            pl.BlockSpec((DIFF_V_DIM, 1), lambda b, h, sc: (0, 0)),
            pl.BlockSpec(c_pad.shape, lambda b, h, sc: (0, 0)),
            pl.BlockSpec((w_ada.shape[0], ada_tn),
                         lambda b, h, sc: (0, ada_blk0 + b * heads + h)),
            pl.BlockSpec((1, ada_tn), lambda b, h, sc: (0, ada_blk0 + b * heads + h)),
        ],
        out_specs=[pl.BlockSpec((None, s, LANES), lambda b, h, sc: (b, 0, h)),
                   pl.BlockSpec((8, ada_tn), lambda b, h, sc: (0, b * heads + h))],
        scratch_shapes=[pltpu.VMEM((1, 2 * t), F32),
                        pltpu.VMEM((V_AUG_ROWS, 2 * t), F32),
                        pltpu.VMEM((nt, V_AUG_ROWS, 2 * t), F32),
                        pltpu.VMEM((t, 2 * t), F32), pltpu.VMEM((t, 2 * t), F32),
                        pltpu.VMEM((1, 2 * t), F32), pltpu.VMEM((1, 2 * t), F32)],
    )
    return pl.pallas_call(
        functools.partial(_attn_kernel, lam_init, n_steps),
        grid_spec=grid_spec,
        out_shape=[jax.ShapeDtypeStruct((bsz, s, heads * DIFF_V_DIM), BF16),
                   jax.ShapeDtypeStruct((8, n_tail), F32)],
        compiler_params=_params("arbitrary", "arbitrary"),
        name="diff_attn",
    )(sched, qq, k, vt, _attn_mask_bias(t), lq1.reshape(1, -1), lk1.reshape(1, -1),
      lq2.reshape(1, -1), lk2.reshape(1, -1), sub_g.reshape(-1, 1),
      c_pad, w_ada, b_ada.reshape(1, -1))


def _out_kernel(alpha, x_ref, gt_ref, a_ref, b_ref, wa_ref, wb_ref, lg_ref, lb_ref,
                o_ref):
    gate = 1.0 + gt_ref[...]
    for c in range(x_ref.shape[0] // ROW_CHUNK):
        rows = pl.ds(c * ROW_CHUNK, ROW_CHUNK)
        y = jnp.dot(a_ref[rows, :], wa_ref[...], preferred_element_type=F32)
        y = y + jnp.dot(b_ref[rows, :], wb_ref[...], preferred_element_type=F32)
        z = alpha * x_ref[rows, :] + gate * y
        o_ref[rows, :] = _layer_norm_rows(z, lg_ref[...], lb_ref[...])


def _out_ln(x, gate, out_a, out_b, w_out, ln_g, ln_b, alpha):
    bsz, s, d = x.shape
    half = out_a.shape[-1]
    tile = pl.BlockSpec((None, MIX_TM, d), lambda b, i: (b, i, 0))
    act = pl.BlockSpec((None, MIX_TM, half), lambda b, i: (b, i, 0))
    row = pl.BlockSpec((1, d), lambda b, i: (0, 0))
    return pl.pallas_call(
        functools.partial(_out_kernel, alpha),
        grid=(bsz, s // MIX_TM),
        in_specs=[tile, pl.BlockSpec((None, 1, d), lambda b, i: (b, 0, 0)), act, act,
                  pl.BlockSpec((half, d), lambda b, i: (0, 0)),
                  pl.BlockSpec((half, d), lambda b, i: (1, 0)),
                  row, row],
        out_specs=tile,
        out_shape=jax.ShapeDtypeStruct((bsz, s, d), F32),
        compiler_params=_params("arbitrary", "arbitrary"),
        name="out_ln",
    )(x, gate, out_a, out_b, w_out, w_out, ln_g.reshape(1, d), ln_b.reshape(1, d))


def _rope_tables(s):
    half = DIFF_QK_DIM // 2
    pos = np.arange(s, dtype=np.float64)
    inv_freq = ROPE_THETA ** (-np.arange(0, DIFF_QK_DIM, 2, dtype=np.float64) / DIFF_QK_DIM)
    ang = pos[:, None] * inv_freq[None, :]
    cos, sin = np.cos(ang), np.sin(ang)
    cos = np.concatenate([cos, cos, cos, cos], axis=-1)
    sin_signed = np.concatenate([-sin, sin, -sin, sin], axis=-1)
    assert cos.shape[-1] == 4 * half == LANES
    return jnp.asarray(cos, dtype=F32), jnp.asarray(sin_signed, dtype=F32)


def kernel(x, c, w_ada, b_ada, ffn1_w_gu, ffn1_w_down, ln1_g, ln1_b, w_in, gmlp_ln_g, gmlp_ln_b, gmlp_w_s, gmlp_b_s, lambda_q1, lambda_k1, lambda_q2, lambda_k2, diff_subln_g, w_out, ln2_g, ln2_b, ffn2_w_gu, ffn2_w_down, ln3_g, ln3_b):
    bsz, s, d = x.shape
    depth = w_ada.shape[0]
    alpha = (2 * depth) ** 0.25
    gw = gmlp_ln_g.shape[-1]
    qk = DIFF_HEADS * 2 * DIFF_QK_DIM
    cos, sin_signed = _rope_tables(s)
    c_pad = jnp.zeros((8, d), F32).at[:bsz].set(c)

    assert qk == gw and w_in.shape[-1] == 5 * gw
    n_early = 5

    for l in range(depth):
        mod_early = _adaln(c_pad, w_ada[l], b_ada[l], n_early * d)[:bsz]
        s1, sc1, g1, s2, sc2 = [m[:, None, :] for m in jnp.split(mod_early, n_early, axis=-1)]

        x, w_in_b = _ffn_ln_f32_weights(x, s1, sc1, g1, ffn1_w_gu[l], ffn1_w_down[l],
                                        ln1_g[l], ln1_b[l], alpha, side_cast=w_in[l])
        out_a, w_out_b, w_gu2_b, w_down2_b = _gmlp(
            x, s2, sc2, w_in_b, gw, gmlp_ln_g[l], gmlp_ln_b[l],
            gmlp_w_s[l], jnp.transpose(gmlp_b_s[l]),
            cast_weights=(w_out[l], ffn2_w_gu[l], ffn2_w_down[l]))
        qq, k, vt = _qkv_rope(x, s2, sc2, w_in_b, gw, cos, sin_signed)
        lam_init = 0.8 - 0.6 * math.exp(-0.3 * l)
        out_b, mod_late = _diff_attn(qq, k, vt, lambda_q1[l], lambda_k1[l], lambda_q2[l],
                                     lambda_k2[l], diff_subln_g[l], lam_init,
                                     c_pad, w_ada[l], b_ada[l], n_early * d)
        g2, s3, sc3, g3 = [m[:, None, :]
                           for m in jnp.split(mod_late[:bsz], N_MOD - n_early, axis=-1)]
        x = _out_ln(x, g2, out_a, out_b, w_out_b, ln2_g[l], ln2_b[l], alpha)

        x = _ffn_ln(x, s3, sc3, g3, w_gu2_b, w_down2_b, ln3_g[l], ln3_b[l], alpha)
    return x
```

```python
import functools
import math

import jax
import jax.numpy as jnp
import numpy as np
from jax import lax
from jax.experimental import pallas as pl
from jax.experimental.pallas import tpu as pltpu

F32 = jnp.float32
BF16 = jnp.bfloat16

CHUNK = 64
GMLP_BLOCK = 128
GMLP_GROUPS = 8
DIFF_HEADS = 8
DIFF_V_DIM = 128
DIFF_QK_DIM = 64
N_MOD = 9
ROPE_THETA = 10000.0
LN_EPS = 1e-5
NEG_INF = -1e30
LOG2_E = 1.4426950408889634

SUBLANES = 8
BF16_SUBLANE_ROWS = 2 * SUBLANES
LANES = 128
VMEM_LIMIT_BYTES = 56 * 1024 * 1024
V_AUG_ROWS = DIFF_V_DIM + BF16_SUBLANE_ROWS
FFN_VMEM_LIMIT_BYTES = 61 * 1024 * 1024

ADA_TN = 1024
FFN_TM = 1024
FFN_TF = 512
FFN_TF_F32 = 256
FFN_UP_COLS = 256
FFN_DOWN_COLS = 256
CAST_BLOCK = (128, 512)
MIX_TM = 512
ROW_CHUNK = 256
PROJ_COLS = 256
ATT_T = 512
ATT_UNROLL = 34


def _params(*sem, vmem_limit_bytes=VMEM_LIMIT_BYTES):
    return pltpu.CompilerParams(dimension_semantics=sem,
                                vmem_limit_bytes=vmem_limit_bytes)


def _layer_norm_rows(z, g, b):
    mu = jnp.mean(z, axis=-1, keepdims=True)
    zc = z - mu
    var = jnp.mean(zc * zc, axis=-1, keepdims=True)
    return zc * lax.rsqrt(var + LN_EPS) * g + b


def _adaln_block(c_ref, w_ref, b_ref):
    c = c_ref[...]
    ca = (c / (1.0 + jnp.exp(-c))).astype(BF16)
    return jnp.dot(ca, w_ref[...].astype(BF16), preferred_element_type=F32) + b_ref[...]


def _adaln_kernel(c_ref, w_ref, b_ref, o_ref):
    o_ref[...] = _adaln_block(c_ref, w_ref, b_ref)


def _adaln(c_pad, w, b, n_cols):
    d = w.shape[0]
    return pl.pallas_call(
        _adaln_kernel,
        grid=(n_cols // ADA_TN,),
        in_specs=[pl.BlockSpec(c_pad.shape, lambda j: (0, 0)),
                  pl.BlockSpec((d, ADA_TN), lambda j: (0, j)),
                  pl.BlockSpec((1, ADA_TN), lambda j: (0, j))],
        out_specs=pl.BlockSpec((SUBLANES, ADA_TN), lambda j: (0, j)),
        out_shape=jax.ShapeDtypeStruct((SUBLANES, n_cols), F32),
        compiler_params=_params("arbitrary"),
        name="adaln",
    )(c_pad, w, b.reshape(1, -1))


def _ffn_kernel(alpha, side_cast, own_cast, aliased, x_ref, sh_ref, sc_ref, gt_ref,
                wg_ref, wu_ref, wd_ref, lg_ref, lb_ref, *rest):
    rest = list(rest)
    cast_in = rest.pop(0) if side_cast else None
    if aliased:
        rest.pop(0)
    o_ref = rest.pop(0)
    casts = []
    if side_cast:
        casts.append((cast_in, rest.pop(0)))
    if own_cast:
        copies = rest[:3]
        rest = rest[3:]
        casts += list(zip((wg_ref, wu_ref, wd_ref), copies))
        wg_ref, wu_ref, wd_ref = copies
    h_ref, a_ref = rest
    j = pl.program_id(1)
    last = pl.num_programs(1) - 1

    def run_casts():
        for src, dst in casts:
            dst[...] = src[...].astype(BF16)

    def hidden():
        h = h_ref[...]
        for c in range(a_ref.shape[1] // FFN_UP_COLS):
            cols = slice(c * FFN_UP_COLS, (c + 1) * FFN_UP_COLS)
            g = jnp.dot(h, wg_ref[:, cols], preferred_element_type=F32)
            u = jnp.dot(h, wu_ref[:, cols], preferred_element_type=F32)
            a_ref[:, cols] = (g / (1.0 + jnp.exp(-g)) * u).astype(BF16)

    def down(first):
        a = a_ref[...]
        for c in range(o_ref.shape[1] // FFN_DOWN_COLS):
            cols = slice(c * FFN_DOWN_COLS, (c + 1) * FFN_DOWN_COLS)
            y = jnp.dot(a, wd_ref[:, cols], preferred_element_type=F32)
            o_ref[:, cols] = y if first else o_ref[:, cols] + y

    @pl.when(j == 0)
    def _():
        run_casts()
        h_ref[...] = (x_ref[...] * (1.0 + sc_ref[...]) + sh_ref[...]).astype(BF16)
        hidden()
        down(first=True)

    @pl.when(jnp.logical_and(j > 0, j < last))
    def _():
        run_casts()
        hidden()
        down(first=False)

    @pl.when(j == last)
    def _():
        run_casts()
        hidden()
        gate = 0.5 * (1.0 + gt_ref[...])
        for r in range(o_ref.shape[0] // ROW_CHUNK):
            rows = pl.ds(r * ROW_CHUNK, ROW_CHUNK)
            y = o_ref[rows, :] + jnp.dot(a_ref[rows, :], wd_ref[...],
                                         preferred_element_type=F32)
            z = alpha * x_ref[rows, :] + gate * y
            o_ref[rows, :] = _layer_norm_rows(z, lg_ref[...], lb_ref[...])


def _ffn_call(x, shift, scale, gate, w_gate, gate_blk0, w_up, up_blk0, w_down, ln_g, ln_b,
              alpha, *, tf, tile0, n_tiles, side_cast=None, own_cast=False, prev_out=None):
    bsz, s, d = x.shape
    f = w_down.shape[0]
    nf = f // tf
    nti = s // FFN_TM

    def token_tile(t):
        return (t + tile0) // nti, (t + tile0) % nti

    vec = pl.BlockSpec((None, 1, d), lambda t, j: (token_tile(t)[0], 0, 0))
    row = pl.BlockSpec((1, d), lambda t, j: (0, 0))
    tile = pl.BlockSpec((None, FFN_TM, d), lambda t, j: (*token_tile(t), 0))
    in_specs = [tile, vec, vec, vec,
                pl.BlockSpec((d, tf), lambda t, j: (0, gate_blk0 + j)),
                pl.BlockSpec((d, tf), lambda t, j: (0, up_blk0 + j)),
                pl.BlockSpec((tf, d), lambda t, j: (j, 0)),
                row, row]
    out_specs, out_shape = [tile], [jax.ShapeDtypeStruct((bsz, s, d), F32)]
    operands = [x, shift, scale, gate, w_gate, w_up, w_down,
                ln_g.reshape(1, d), ln_b.reshape(1, d)]
    aliases = {}
    if side_cast is not None:
        nrb, ncb = side_cast.shape[0] // CAST_BLOCK[0], side_cast.shape[1] // CAST_BLOCK[1]
        assert (nrb * CAST_BLOCK[0], ncb * CAST_BLOCK[1]) == side_cast.shape
        assert nrb * ncb <= n_tiles * nf

        def cast_index(t, j):
            blk = jnp.minimum(t * nf + j, nrb * ncb - 1)
            return blk // ncb, blk % ncb

        cast_spec = pl.BlockSpec(CAST_BLOCK, cast_index)
        in_specs.append(cast_spec)
        out_specs.append(cast_spec)
        out_shape.append(jax.ShapeDtypeStruct(side_cast.shape, BF16))
        operands.append(side_cast)
    if prev_out is not None:
        in_specs.append(pl.BlockSpec(memory_space=pl.ANY))
        operands.append(prev_out)
        aliases = {len(operands) - 1: 0}
    if own_cast:
        out_specs += [pl.BlockSpec((d, tf), lambda t, j: (0, j)),
                      pl.BlockSpec((d, tf), lambda t, j: (0, j)),
                      pl.BlockSpec((tf, d), lambda t, j: (j, 0))]
        out_shape += [jax.ShapeDtypeStruct((d, f), BF16), jax.ShapeDtypeStruct((d, f), BF16),
                      jax.ShapeDtypeStruct((f, d), BF16)]
    return pl.pallas_call(
        functools.partial(_ffn_kernel, alpha, side_cast is not None, own_cast,
                          prev_out is not None),
        grid=(n_tiles, nf),
        in_specs=in_specs,
        out_specs=out_specs,
        out_shape=out_shape,
        input_output_aliases=aliases,
        scratch_shapes=[pltpu.VMEM((FFN_TM, d), BF16), pltpu.VMEM((FFN_TM, tf), BF16)],
        compiler_params=_params("arbitrary", "arbitrary",
                                vmem_limit_bytes=FFN_VMEM_LIMIT_BYTES),
        name="ffn_ln",
    )(*operands)


def _ffn_ln(x, shift, scale, gate, w_gu, w_down, ln_g, ln_b, alpha):
    bsz, s, _ = x.shape
    nf = w_down.shape[0] // FFN_TF
    return _ffn_call(x, shift, scale, gate, w_gu, 0, w_gu, nf, w_down, ln_g, ln_b, alpha,
                     tf=FFN_TF, tile0=0, n_tiles=bsz * (s // FFN_TM))[0]


def _ffn_ln_f32_weights(x, shift, scale, gate, w_gu, w_down, ln_g, ln_b, alpha, side_cast):
    bsz, s, _ = x.shape
    n_tiles = bsz * (s // FFN_TM)
    nf_first = w_down.shape[0] // FFN_TF_F32
    x_first, w_gate_b, w_up_b, w_down_b = _ffn_call(
        x, shift, scale, gate, w_gu, 0, w_gu, nf_first, w_down, ln_g, ln_b, alpha,
        tf=FFN_TF_F32, tile0=0, n_tiles=1, own_cast=True)
    x_out, side_b = _ffn_call(
        x, shift, scale, gate, w_gate_b, 0, w_up_b, 0, w_down_b, ln_g, ln_b, alpha,
        tf=FFN_TF, tile0=1, n_tiles=n_tiles - 1, side_cast=side_cast, prev_out=x_first)
    return x_out, side_b


def _gmlp_kernel(n_cast, x_ref, sh_ref, sc_ref, wu_ref, wv_ref, lg_ref, lb_ref, ws_ref,
                 bst_ref, *rest):
    cast_in, o_ref, cast_out = rest[:n_cast], rest[n_cast], rest[n_cast + 1:]
    for src, dst in zip(cast_in, cast_out):
        dst[...] = src[...].astype(BF16)

    pi = lax.broadcasted_iota(jnp.int32, (GMLP_BLOCK, GMLP_BLOCK), 0) // CHUNK
    pj = lax.broadcasted_iota(jnp.int32, (GMLP_BLOCK, GMLP_BLOCK), 1) // CHUNK
    keep = pj <= pi
    bst = bst_ref[...]

    h = (x_ref[...] * (1.0 + sc_ref[...]) + sh_ref[...]).astype(BF16)

    def gelu_proj(w_ref):
        parts = []
        for c in range(w_ref.shape[1] // PROJ_COLS):
            cols = slice(c * PROJ_COLS, (c + 1) * PROJ_COLS)
            parts.append(jax.nn.gelu(jnp.dot(h, w_ref[:, cols], preferred_element_type=F32)))
        return jnp.concatenate(parts, axis=1)

    v = _layer_norm_rows(gelu_proj(wv_ref), lg_ref[...], lb_ref[...]).astype(BF16)
    u = gelu_proj(wu_ref)
    n_blk = x_ref.shape[0] // GMLP_BLOCK
    for g in range(GMLP_GROUPS):
        w = jnp.where(keep, ws_ref[g], 0.0).astype(BF16)
        bias = bst[:, g:g + 1]
        cols = slice(g * LANES, (g + 1) * LANES)
        v_blocks = jnp.concatenate(
            [v[blk * GMLP_BLOCK:(blk + 1) * GMLP_BLOCK, cols] for blk in range(n_blk)], axis=1)
        mixed = jnp.dot(w, v_blocks, preferred_element_type=F32) + bias
        for blk in range(n_blk):
            rows = slice(blk * GMLP_BLOCK, (blk + 1) * GMLP_BLOCK)
            o_ref[rows, cols] = (u[rows, cols]
                                 * mixed[:, blk * LANES:(blk + 1) * LANES]).astype(BF16)


def _gmlp(x, shift, scale, w_in, width, ln_g, ln_b, w_s, b_s_t, cast_weights):
    bsz, s, d = x.shape
    nt = s // MIX_TM
    vec = pl.BlockSpec((None, 1, d), lambda b, i: (b, 0, 0))
    full = lambda shape: pl.BlockSpec(shape, lambda b, i: (0,) * len(shape))
    w_cols = lambda blk: pl.BlockSpec((d, width), lambda b, i: (0, blk))
    slab_specs = []
    for w in cast_weights:
        rows = w.shape[0] // (bsz * nt)
        assert rows * bsz * nt == w.shape[0] and rows % BF16_SUBLANE_ROWS == 0, w.shape
        slab_specs.append(pl.BlockSpec((rows, w.shape[1]), lambda b, i: (b * nt + i, 0)))
    return pl.pallas_call(
        functools.partial(_gmlp_kernel, len(cast_weights)),
        grid=(bsz, nt),
        in_specs=[pl.BlockSpec((None, MIX_TM, d), lambda b, i: (b, i, 0)), vec, vec,
                  w_cols(0), w_cols(1),
                  full((1, width)), full((1, width)),
                  full(w_s.shape), full(b_s_t.shape)] + slab_specs,
        out_specs=[pl.BlockSpec((None, MIX_TM, width), lambda b, i: (b, i, 0))] + slab_specs,
        out_shape=[jax.ShapeDtypeStruct((bsz, s, width), BF16)]
        + [jax.ShapeDtypeStruct(w.shape, BF16) for w in cast_weights],
        compiler_params=_params("arbitrary", "arbitrary"),
        name="gmlp",
    )(x, shift, scale, w_in, w_in, ln_g.reshape(1, width), ln_b.reshape(1, width),
      w_s, b_s_t, *cast_weights)


def _rope(x, cos, sin_signed, first_half):
    rot = jnp.where(first_half, pltpu.roll(x, 3 * LANES // 4, 1),
                    pltpu.roll(x, LANES // 4, 1))
    return x * cos + rot * sin_signed


def _qkv_kernel(x_ref, sh_ref, sc_ref, wq_ref, wk_ref, wv_ref, cos_ref, sin_ref,
                qq_ref, k_ref, vt_ref):
    t = x_ref.shape[0]
    h = (x_ref[...] * (1.0 + sc_ref[...]) + sh_ref[...]).astype(BF16)
    cos = cos_ref[...]
    sin = sin_ref[...]
    lane = lax.broadcasted_iota(jnp.int32, (t, LANES), 1)
    first_half = (lane % DIFF_QK_DIM) < (DIFF_QK_DIM // 2)
    comp = lax.broadcasted_iota(jnp.int32, (LANES, t), 0) < DIFF_QK_DIM
    q_scale = DIFF_QK_DIM ** -0.5 * LOG2_E
    ones = jnp.ones((V_AUG_ROWS - DIFF_V_DIM, t), BF16)
    row_chunk = lax.broadcasted_iota(jnp.int32, (t, LANES), 0) // CHUNK
    chunk_onehot = jnp.where(row_chunk == lane, 1.0, 0.0).astype(BF16)
    q = jnp.dot(h, wq_ref[...], preferred_element_type=F32)
    k = jnp.dot(h, wk_ref[...], preferred_element_type=F32)
    v = jnp.dot(h, wv_ref[...], preferred_element_type=F32)
    for hd in range(DIFF_HEADS):
        cols = slice(hd * LANES, (hd + 1) * LANES)
        k_ref[hd, :, :LANES] = _rope(k[:, cols], cos, sin, first_half).astype(BF16)
        k_ref[hd, :, LANES:] = chunk_onehot
        qt = (_rope(q[:, cols], cos, sin, first_half) * q_scale).T
        qq_ref[hd, 0, :, :t] = jnp.where(comp, qt, 0.0).astype(BF16)
        qq_ref[hd, 0, :, t:] = jnp.where(comp, 0.0, qt).astype(BF16)
        vt_ref[hd, 0, :DIFF_V_DIM, :] = v[:, cols].T.astype(BF16)
        vt_ref[hd, 0, DIFF_V_DIM:, :] = ones


def _qkv_rope(x, shift, scale, w_in, width, cos, sin_signed):
    bsz, s, d = x.shape
    t = ATT_T
    nt = s // t
    vec = pl.BlockSpec((None, 1, d), lambda b, i: (b, 0, 0))
    w_cols = lambda blk: pl.BlockSpec((d, width), lambda b, i: (0, blk))
    tab = pl.BlockSpec((t, LANES), lambda b, i: (i, 0))
    return pl.pallas_call(
        _qkv_kernel,
        grid=(bsz, nt),
        in_specs=[pl.BlockSpec((None, t, d), lambda b, i: (b, i, 0)), vec, vec,
                  w_cols(2), w_cols(3), w_cols(4), tab, tab],
        out_specs=[
            pl.BlockSpec((None, DIFF_HEADS, 1, LANES, 2 * t), lambda b, i: (b, 0, i, 0, 0)),
            pl.BlockSpec((None, DIFF_HEADS, t, 2 * LANES), lambda b, i: (b, 0, i, 0)),
            pl.BlockSpec((None, DIFF_HEADS, 1, V_AUG_ROWS, t), lambda b, i: (b, 0, i, 0, 0)),
        ],
        out_shape=[
            jax.ShapeDtypeStruct((bsz, DIFF_HEADS, nt, LANES, 2 * t), BF16),
            jax.ShapeDtypeStruct((bsz, DIFF_HEADS, s, 2 * LANES), BF16),
            jax.ShapeDtypeStruct((bsz, DIFF_HEADS, nt, V_AUG_ROWS, t), BF16),
        ],
        compiler_params=_params("arbitrary", "arbitrary"),
        name="qkv_rope",
    )(x, shift, scale, w_in, w_in, w_in, cos, sin_signed)


def _attn_kernel(lam_init, n_steps, sched_ref, qq_ref, k_ref, vt_ref, bias_ref,
                 lq1_ref, lk1_ref, lq2_ref, lk2_ref, sg_ref, c_ref, wada_ref, bada_ref,
                 o_ref, mod_ref,
                 m_ref, acc_ref, fin_ref, s0_ref, s1_ref, smax0_ref, smax1_ref):
    t = ATT_T
    mod_ref[...] = _adaln_block(c_ref, wada_ref, bada_ref)

    m_ref[...] = jnp.full_like(m_ref, NEG_INF)
    acc_ref[...] = jnp.zeros_like(acc_ref)
    buffers = ((s0_ref, smax0_ref), (s1_ref, smax1_ref))

    def produce(n, s_ref, smax_ref):
        qi, kj, dg = sched_ref[0, n], sched_ref[1, n], sched_ref[2, n]
        rhs = jnp.concatenate([qq_ref[qi], bias_ref[dg]], axis=0)
        kblk = k_ref[pl.ds(pl.multiple_of(kj * t, t), t), :]
        s = jnp.dot(kblk, rhs, preferred_element_type=F32)
        s_ref[...] = s
        smax_ref[...] = jnp.max(s, axis=0, keepdims=True)

    def consume(n, s_ref, smax_ref):
        qi, kj = sched_ref[0, n], sched_ref[1, n]
        s = s_ref[...]
        m_prev = jnp.where(kj == 0, NEG_INF, m_ref[...])
        m_new = jnp.maximum(m_prev, smax_ref[...])
        alpha = jnp.exp2(m_prev - m_new)
        p = jnp.exp2(s - m_new).astype(BF16)
        pv = jnp.dot(vt_ref[kj], p, preferred_element_type=F32)
        acc = alpha * acc_ref[...] + pv
        acc_ref[...] = acc
        fin_ref[qi] = acc
        m_ref[...] = m_new

    produce(0, *buffers[0])
    produce(1, *buffers[1])

    def trip(g, carry):
        for u in range(ATT_UNROLL):
            n = g * ATT_UNROLL + u
            consume(n, *buffers[u % 2])
            produce(n + 2, *buffers[u % 2])
        return carry

    lax.fori_loop(0, n_steps // ATT_UNROLL, trip, 0)

    lam = (jnp.exp(jnp.sum(lq1_ref[...] * lk1_ref[...], keepdims=True))
           - jnp.exp(jnp.sum(lq2_ref[...] * lk2_ref[...], keepdims=True))
           + lam_init)
    sub_g = sg_ref[...] * (1.0 - lam_init)

    for qi in range(fin_ref.shape[0]):
        acc = fin_ref[qi]
        o = acc[:DIFF_V_DIM, :] / acc[DIFF_V_DIM:DIFF_V_DIM + 1, :]
        o = o[:, :t] - lam * o[:, t:]
        ms = jnp.mean(o * o, axis=0, keepdims=True)
        o = o * lax.rsqrt(ms + LN_EPS) * sub_g
        o_ref[qi * t:(qi + 1) * t, :] = o.T.astype(BF16)


def _attn_schedule(nt):
    steps = [(qi, kj, int(kj == qi)) for qi in range(nt) for kj in range(qi + 1)]
    n_steps = len(steps)
    steps += [(0, 0, 1)] * 2
    return jnp.asarray(steps, jnp.int32).T, n_steps


def _attn_mask_bias(t):
    key_chunk = np.arange(LANES)[:, None]
    q_chunk = (np.arange(2 * t)[None, :] % t) // CHUNK
    hidden = (key_chunk < t // CHUNK) & (key_chunk > q_chunk)
    diag = np.where(hidden, NEG_INF, 0.0).astype(np.float32)
    return jnp.asarray(np.stack([np.zeros_like(diag), diag]), dtype=BF16)


def _diff_attn(qq, k, vt, lq1, lk1, lq2, lk2, sub_g, lam_init, c_pad, w_ada, b_ada, ada_col0):
    bsz, heads, nt, _, _ = qq.shape
    t = ATT_T
    s = nt * t
    sched, n_steps = _attn_schedule(nt)
    assert n_steps % ATT_UNROLL == 0 and ATT_UNROLL % 2 == 0
    n_tail = w_ada.shape[1] - ada_col0
    ada_tn = n_tail // (bsz * heads)
    assert ada_tn * bsz * heads == n_tail and ada_tn % LANES == 0 and ada_col0 % ada_tn == 0
    ada_blk0 = ada_col0 // ada_tn
    lam_spec = pl.BlockSpec((1, DIFF_QK_DIM), lambda b, h, sc: (0, 0))
    grid_spec = pltpu.PrefetchScalarGridSpec(
        num_scalar_prefetch=1,
        grid=(bsz, heads),
        in_specs=[
            pl.BlockSpec((None, None, nt, LANES, 2 * t), lambda b, h, sc: (b, h, 0, 0, 0)),
            pl.BlockSpec((None, None, s, 2 * LANES), lambda b, h, sc: (b, h, 0, 0)),
            pl.BlockSpec((None, None, nt, V_AUG_ROWS, t), lambda b, h, sc: (b, h, 0, 0, 0)),
            pl.BlockSpec((2, LANES, 2 * t), lambda b, h, sc: (0, 0, 0)),
            lam_spec, lam_spec, lam_spec, lam_spec,
            pl.BlockSpec((DIFF_V_DIM, 1), lambda b, h, sc: (0, 0)),
            pl.BlockSpec(c_pad.shape, lambda b, h, sc: (0, 0)),
            pl.BlockSpec((w_ada.shape[0], ada_tn),
                         lambda b, h, sc: (0, ada_blk0 + b * heads + h)),
            pl.BlockSpec((1, ada_tn), lambda b, h, sc: (0, ada_blk0 + b * heads + h)),
        ],
        out_specs=[pl.BlockSpec((None, s, LANES), lambda b, h, sc: (b, 0, h)),
                   pl.BlockSpec((SUBLANES, ada_tn), lambda b, h, sc: (0, b * heads + h))],
        scratch_shapes=[pltpu.VMEM((1, 2 * t), F32),
                        pltpu.VMEM((V_AUG_ROWS, 2 * t), F32),
                        pltpu.VMEM((nt, V_AUG_ROWS, 2 * t), F32),
                        pltpu.VMEM((t, 2 * t), F32), pltpu.VMEM((t, 2 * t), F32),
                        pltpu.VMEM((1, 2 * t), F32), pltpu.VMEM((1, 2 * t), F32)],
    )
    return pl.pallas_call(
        functools.partial(_attn_kernel, lam_init, n_steps),
        grid_spec=grid_spec,
        out_shape=[jax.ShapeDtypeStruct((bsz, s, heads * DIFF_V_DIM), BF16),
                   jax.ShapeDtypeStruct((SUBLANES, n_tail), F32)],
        compiler_params=_params("arbitrary", "arbitrary"),
        name="diff_attn",
    )(sched, qq, k, vt, _attn_mask_bias(t), lq1.reshape(1, -1), lk1.reshape(1, -1),
      lq2.reshape(1, -1), lk2.reshape(1, -1), sub_g.reshape(-1, 1),
      c_pad, w_ada, b_ada.reshape(1, -1))


def _out_kernel(alpha, x_ref, gt_ref, a_ref, b_ref, wa_ref, wb_ref, lg_ref, lb_ref,
                o_ref):
    gate = 1.0 + gt_ref[...]
    for c in range(x_ref.shape[0] // ROW_CHUNK):
        rows = pl.ds(c * ROW_CHUNK, ROW_CHUNK)
        y = jnp.dot(a_ref[rows, :], wa_ref[...], preferred_element_type=F32)
        y = y + jnp.dot(b_ref[rows, :], wb_ref[...], preferred_element_type=F32)
        z = alpha * x_ref[rows, :] + gate * y
        o_ref[rows, :] = _layer_norm_rows(z, lg_ref[...], lb_ref[...])


def _out_ln(x, gate, out_a, out_b, w_out, ln_g, ln_b, alpha):
    bsz, s, d = x.shape
    half = out_a.shape[-1]
    tile = pl.BlockSpec((None, MIX_TM, d), lambda b, i: (b, i, 0))
    act = pl.BlockSpec((None, MIX_TM, half), lambda b, i: (b, i, 0))
    row = pl.BlockSpec((1, d), lambda b, i: (0, 0))
    return pl.pallas_call(
        functools.partial(_out_kernel, alpha),
        grid=(bsz, s // MIX_TM),
        in_specs=[tile, pl.BlockSpec((None, 1, d), lambda b, i: (b, 0, 0)), act, act,
                  pl.BlockSpec((half, d), lambda b, i: (0, 0)),
                  pl.BlockSpec((half, d), lambda b, i: (1, 0)),
                  row, row],
        out_specs=tile,
        out_shape=jax.ShapeDtypeStruct((bsz, s, d), F32),
        compiler_params=_params("arbitrary", "arbitrary"),
        name="out_ln",
    )(x, gate, out_a, out_b, w_out, w_out, ln_g.reshape(1, d), ln_b.reshape(1, d))


def _rope_tables(s):
    half = DIFF_QK_DIM // 2
    pos = np.arange(s, dtype=np.float64)
    inv_freq = ROPE_THETA ** (-np.arange(0, DIFF_QK_DIM, 2, dtype=np.float64) / DIFF_QK_DIM)
    ang = pos[:, None] * inv_freq[None, :]
    cos, sin = np.cos(ang), np.sin(ang)
    cos = np.concatenate([cos, cos, cos, cos], axis=-1)
    sin_signed = np.concatenate([-sin, sin, -sin, sin], axis=-1)
    assert cos.shape[-1] == 4 * half == LANES
    return jnp.asarray(cos, dtype=F32), jnp.asarray(sin_signed, dtype=F32)


def kernel(x, c, w_ada, b_ada, ffn1_w_gu, ffn1_w_down, ln1_g, ln1_b, w_in, gmlp_ln_g, gmlp_ln_b, gmlp_w_s, gmlp_b_s, lambda_q1, lambda_k1, lambda_q2, lambda_k2, diff_subln_g, w_out, ln2_g, ln2_b, ffn2_w_gu, ffn2_w_down, ln3_g, ln3_b):
    bsz, s, d = x.shape
    depth = w_ada.shape[0]
    alpha = (2 * depth) ** 0.25
    gw = gmlp_ln_g.shape[-1]
    qk = DIFF_HEADS * 2 * DIFF_QK_DIM
    cos, sin_signed = _rope_tables(s)
    assert bsz <= SUBLANES
    c_pad = jnp.zeros((SUBLANES, d), F32).at[:bsz].set(c)

    assert qk == gw and w_in.shape[-1] == 5 * gw
    n_early = 5

    for l in range(depth):
        mod_early = _adaln(c_pad, w_ada[l], b_ada[l], n_early * d)[:bsz]
        s1, sc1, g1, s2, sc2 = [m[:, None, :] for m in jnp.split(mod_early, n_early, axis=-1)]

        x, w_in_b = _ffn_ln_f32_weights(x, s1, sc1, g1, ffn1_w_gu[l], ffn1_w_down[l],
                                        ln1_g[l], ln1_b[l], alpha, side_cast=w_in[l])
        out_a, w_out_b, w_gu2_b, w_down2_b = _gmlp(
            x, s2, sc2, w_in_b, gw, gmlp_ln_g[l], gmlp_ln_b[l],
            gmlp_w_s[l], jnp.transpose(gmlp_b_s[l]),
            cast_weights=(w_out[l], ffn2_w_gu[l], ffn2_w_down[l]))
        qq, k, vt = _qkv_rope(x, s2, sc2, w_in_b, gw, cos, sin_signed)
        lam_init = 0.8 - 0.6 * math.exp(-0.3 * l)
        out_b, mod_late = _diff_attn(qq, k, vt, lambda_q1[l], lambda_k1[l], lambda_q2[l],
                                     lambda_k2[l], diff_subln_g[l], lam_init,
                                     c_pad, w_ada[l], b_ada[l], n_early * d)
        g2, s3, sc3, g3 = [m[:, None, :]
                           for m in jnp.split(mod_late[:bsz], N_MOD - n_early, axis=-1)]
        x = _out_ln(x, g2, out_a, out_b, w_out_b, ln2_g[l], ln2_b[l], alpha)

        x = _ffn_ln(x, s3, sc3, g3, w_gu2_b, w_down2_b, ln3_g[l], ln3_b[l], alpha)
    return x
```

```python
import functools
import math

import jax
import jax.numpy as jnp
import numpy as np
from jax import lax
from jax.experimental import pallas as pl
from jax.experimental.pallas import tpu as pltpu

F32 = jnp.float32
BF16 = jnp.bfloat16

CHUNK = 64
GMLP_BLOCK = 128
GMLP_GROUPS = 8
DIFF_HEADS = 8
DIFF_V_DIM = 128
DIFF_QK_DIM = 64
N_MOD = 9
ROPE_THETA = 10000.0
LN_EPS = 1e-5
NEG_INF = -1e30
LOG2_E = 1.4426950408889634

SUBLANES = 8
BF16_SUBLANE_ROWS = 2 * SUBLANES
LANES = 128
VMEM_LIMIT_BYTES = 56 * 1024 * 1024
V_AUG_ROWS = DIFF_V_DIM + BF16_SUBLANE_ROWS
FFN_VMEM_LIMIT_BYTES = 61 * 1024 * 1024

ADA_TN = 1024
FFN_TM = 1024
FFN_TF = 512
FFN_TF_F32 = 256
FFN_UP_COLS = 256
FFN_DOWN_COLS = 256
CAST_BLOCK = (128, 512)
MIX_TM = 512
ROW_CHUNK = 256
PROJ_COLS = 256
ATT_T = 512
ATT_UNROLL = 34


def _params(*sem, vmem_limit_bytes=VMEM_LIMIT_BYTES):
    return pltpu.CompilerParams(dimension_semantics=sem,
                                vmem_limit_bytes=vmem_limit_bytes)


def _layer_norm_rows(z, g, b, eps=LN_EPS):
    mu = jnp.mean(z, axis=-1, keepdims=True)
    zc = z - mu
    var = jnp.mean(zc * zc, axis=-1, keepdims=True)
    return zc * lax.rsqrt(var + eps) * g + b


def _residual_layer_norm(alpha, x, gate, y, g, b):
    return _layer_norm_rows(x + (gate * (1.0 / alpha)) * y, g, b, eps=LN_EPS / alpha ** 2)


def _adaln_block(c_ref, w_ref, b_ref):
    c = c_ref[...]
    ca = (c / (1.0 + jnp.exp(-c))).astype(BF16)
    return jnp.dot(ca, w_ref[...].astype(BF16), preferred_element_type=F32) + b_ref[...]


def _adaln_kernel(c_ref, w_ref, b_ref, o_ref):
    o_ref[...] = _adaln_block(c_ref, w_ref, b_ref)


def _adaln(c_pad, w, b, n_cols):
    d = w.shape[0]
    return pl.pallas_call(
        _adaln_kernel,
        grid=(n_cols // ADA_TN,),
        in_specs=[pl.BlockSpec(c_pad.shape, lambda j: (0, 0)),
                  pl.BlockSpec((d, ADA_TN), lambda j: (0, j)),
                  pl.BlockSpec((1, ADA_TN), lambda j: (0, j))],
        out_specs=pl.BlockSpec((SUBLANES, ADA_TN), lambda j: (0, j)),
        out_shape=jax.ShapeDtypeStruct((SUBLANES, n_cols), F32),
        compiler_params=_params("arbitrary"),
        name="adaln",
    )(c_pad, w, b.reshape(1, -1))


def _ffn_kernel(alpha, side_cast, own_cast, aliased, x_ref, sh_ref, sc_ref, gt_ref,
                wg_ref, wu_ref, wd_ref, lg_ref, lb_ref, *rest):
    rest = list(rest)
    cast_in = rest.pop(0) if side_cast else None
    if aliased:
        rest.pop(0)
    o_ref = rest.pop(0)
    casts = []
    if side_cast:
        casts.append((cast_in, rest.pop(0)))
    if own_cast:
        copies = rest[:3]
        rest = rest[3:]
        casts += list(zip((wg_ref, wu_ref, wd_ref), copies))
        wg_ref, wu_ref, wd_ref = copies
    h_ref, a_ref = rest
    j = pl.program_id(1)
    last = pl.num_programs(1) - 1

    def run_casts():
        for src, dst in casts:
            dst[...] = src[...].astype(BF16)

    def hidden():
        h = h_ref[...]
        for c in range(a_ref.shape[1] // FFN_UP_COLS):
            cols = slice(c * FFN_UP_COLS, (c + 1) * FFN_UP_COLS)
            g = jnp.dot(h, wg_ref[:, cols], preferred_element_type=F32)
            u = jnp.dot(h, wu_ref[:, cols], preferred_element_type=F32)
            a_ref[:, cols] = (g / (1.0 + jnp.exp(-g)) * u).astype(BF16)

    def down(first):
        a = a_ref[...]
        for c in range(o_ref.shape[1] // FFN_DOWN_COLS):
            cols = slice(c * FFN_DOWN_COLS, (c + 1) * FFN_DOWN_COLS)
            y = jnp.dot(a, wd_ref[:, cols], preferred_element_type=F32)
            o_ref[:, cols] = y if first else o_ref[:, cols] + y

    @pl.when(j == 0)
    def _():
        run_casts()
        h_ref[...] = (x_ref[...] * (1.0 + sc_ref[...]) + sh_ref[...]).astype(BF16)
        hidden()
        down(first=True)

    @pl.when(jnp.logical_and(j > 0, j < last))
    def _():
        run_casts()
        hidden()
        down(first=False)

    @pl.when(j == last)
    def _():
        run_casts()
        hidden()
        gate = 0.5 * (1.0 + gt_ref[...])
        for r in range(o_ref.shape[0] // ROW_CHUNK):
            rows = pl.ds(r * ROW_CHUNK, ROW_CHUNK)
            y = o_ref[rows, :] + jnp.dot(a_ref[rows, :], wd_ref[...],
                                         preferred_element_type=F32)
            o_ref[rows, :] = _residual_layer_norm(alpha, x_ref[rows, :], gate, y,
                                                  lg_ref[...], lb_ref[...])


def _ffn_call(x, shift, scale, gate, w_gate, gate_blk0, w_up, up_blk0, w_down, ln_g, ln_b,
              alpha, *, tf, tile0, n_tiles, side_cast=None, own_cast=False, prev_out=None):
    bsz, s, d = x.shape
    f = w_down.shape[0]
    nf = f // tf
    nti = s // FFN_TM

    def token_tile(t):
        return (t + tile0) // nti, (t + tile0) % nti

    vec = pl.BlockSpec((None, 1, d), lambda t, j: (token_tile(t)[0], 0, 0))
    row = pl.BlockSpec((1, d), lambda t, j: (0, 0))
    tile = pl.BlockSpec((None, FFN_TM, d), lambda t, j: (*token_tile(t), 0))
    in_specs = [tile, vec, vec, vec,
                pl.BlockSpec((d, tf), lambda t, j: (0, gate_blk0 + j)),
                pl.BlockSpec((d, tf), lambda t, j: (0, up_blk0 + j)),
                pl.BlockSpec((tf, d), lambda t, j: (j, 0)),
                row, row]
    out_specs, out_shape = [tile], [jax.ShapeDtypeStruct((bsz, s, d), F32)]
    operands = [x, shift, scale, gate, w_gate, w_up, w_down,
                ln_g.reshape(1, d), ln_b.reshape(1, d)]
    aliases = {}
    if side_cast is not None:
        nrb, ncb = side_cast.shape[0] // CAST_BLOCK[0], side_cast.shape[1] // CAST_BLOCK[1]
        assert (nrb * CAST_BLOCK[0], ncb * CAST_BLOCK[1]) == side_cast.shape
        assert nrb * ncb <= n_tiles * nf

        def cast_index(t, j):
            blk = jnp.minimum(t * nf + j, nrb * ncb - 1)
            return blk // ncb, blk % ncb

        cast_spec = pl.BlockSpec(CAST_BLOCK, cast_index)
        in_specs.append(cast_spec)
        out_specs.append(cast_spec)
        out_shape.append(jax.ShapeDtypeStruct(side_cast.shape, BF16))
        operands.append(side_cast)
    if prev_out is not None:
        in_specs.append(pl.BlockSpec(memory_space=pl.ANY))
        operands.append(prev_out)
        aliases = {len(operands) - 1: 0}
    if own_cast:
        out_specs += [pl.BlockSpec((d, tf), lambda t, j: (0, j)),
                      pl.BlockSpec((d, tf), lambda t, j: (0, j)),
                      pl.BlockSpec((tf, d), lambda t, j: (j, 0))]
        out_shape += [jax.ShapeDtypeStruct((d, f), BF16), jax.ShapeDtypeStruct((d, f), BF16),
                      jax.ShapeDtypeStruct((f, d), BF16)]
    return pl.pallas_call(
        functools.partial(_ffn_kernel, alpha, side_cast is not None, own_cast,
                          prev_out is not None),
        grid=(n_tiles, nf),
        in_specs=in_specs,
        out_specs=out_specs,
        out_shape=out_shape,
        input_output_aliases=aliases,
        scratch_shapes=[pltpu.VMEM((FFN_TM, d), BF16), pltpu.VMEM((FFN_TM, tf), BF16)],
        compiler_params=_params("arbitrary", "arbitrary",
                                vmem_limit_bytes=FFN_VMEM_LIMIT_BYTES),
        name="ffn_ln",
    )(*operands)


def _ffn_ln(x, shift, scale, gate, w_gu, w_down, ln_g, ln_b, alpha):
    bsz, s, _ = x.shape
    nf = w_down.shape[0] // FFN_TF
    return _ffn_call(x, shift, scale, gate, w_gu, 0, w_gu, nf, w_down, ln_g, ln_b, alpha,
                     tf=FFN_TF, tile0=0, n_tiles=bsz * (s // FFN_TM))[0]


def _ffn_ln_f32_weights(x, shift, scale, gate, w_gu, w_down, ln_g, ln_b, alpha, side_cast):
    bsz, s, _ = x.shape
    n_tiles = bsz * (s // FFN_TM)
    nf_first = w_down.shape[0] // FFN_TF_F32
    x_first, w_gate_b, w_up_b, w_down_b = _ffn_call(
        x, shift, scale, gate, w_gu, 0, w_gu, nf_first, w_down, ln_g, ln_b, alpha,
        tf=FFN_TF_F32, tile0=0, n_tiles=1, own_cast=True)
    x_out, side_b = _ffn_call(
        x, shift, scale, gate, w_gate_b, 0, w_up_b, 0, w_down_b, ln_g, ln_b, alpha,
        tf=FFN_TF, tile0=1, n_tiles=n_tiles - 1, side_cast=side_cast, prev_out=x_first)
    return x_out, side_b


def _gmlp_kernel(n_cast, x_ref, sh_ref, sc_ref, wu_ref, wv_ref, lg_ref, lb_ref, ws_ref,
                 bst_ref, *rest):
    cast_in, o_ref, cast_out = rest[:n_cast], rest[n_cast], rest[n_cast + 1:]
    for src, dst in zip(cast_in, cast_out):
        dst[...] = src[...].astype(BF16)

    pi = lax.broadcasted_iota(jnp.int32, (GMLP_BLOCK, GMLP_BLOCK), 0) // CHUNK
    pj = lax.broadcasted_iota(jnp.int32, (GMLP_BLOCK, GMLP_BLOCK), 1) // CHUNK
    keep = pj <= pi
    bst = bst_ref[...]

    h = (x_ref[...] * (1.0 + sc_ref[...]) + sh_ref[...]).astype(BF16)

    def gelu_proj(w_ref):
        parts = []
        for c in range(w_ref.shape[1] // PROJ_COLS):
            cols = slice(c * PROJ_COLS, (c + 1) * PROJ_COLS)
            parts.append(jax.nn.gelu(jnp.dot(h, w_ref[:, cols], preferred_element_type=F32)))
        return jnp.concatenate(parts, axis=1)

    v = _layer_norm_rows(gelu_proj(wv_ref), lg_ref[...], lb_ref[...]).astype(BF16)
    u = gelu_proj(wu_ref)
    n_blk = x_ref.shape[0] // GMLP_BLOCK
    for g in range(GMLP_GROUPS):
        w = jnp.where(keep, ws_ref[g], 0.0).astype(BF16)
        bias = bst[:, g:g + 1]
        cols = slice(g * LANES, (g + 1) * LANES)
        v_blocks = jnp.concatenate(
            [v[blk * GMLP_BLOCK:(blk + 1) * GMLP_BLOCK, cols] for blk in range(n_blk)], axis=1)
        mixed = jnp.dot(w, v_blocks, preferred_element_type=F32) + bias
        for blk in range(n_blk):
            rows = slice(blk * GMLP_BLOCK, (blk + 1) * GMLP_BLOCK)
            o_ref[rows, cols] = (u[rows, cols]
                                 * mixed[:, blk * LANES:(blk + 1) * LANES]).astype(BF16)


def _gmlp(x, shift, scale, w_in, width, ln_g, ln_b, w_s, b_s_t, cast_weights):
    bsz, s, d = x.shape
    nt = s // MIX_TM
    vec = pl.BlockSpec((None, 1, d), lambda b, i: (b, 0, 0))
    full = lambda shape: pl.BlockSpec(shape, lambda b, i: (0,) * len(shape))
    w_cols = lambda blk: pl.BlockSpec((d, width), lambda b, i: (0, blk))
    slab_specs = []
    for w in cast_weights:
        rows = w.shape[0] // (bsz * nt)
        assert rows * bsz * nt == w.shape[0] and rows % BF16_SUBLANE_ROWS == 0, w.shape
        slab_specs.append(pl.BlockSpec((rows, w.shape[1]), lambda b, i: (b * nt + i, 0)))
    return pl.pallas_call(
        functools.partial(_gmlp_kernel, len(cast_weights)),
        grid=(bsz, nt),
        in_specs=[pl.BlockSpec((None, MIX_TM, d), lambda b, i: (b, i, 0)), vec, vec,
                  w_cols(0), w_cols(1),
                  full((1, width)), full((1, width)),
                  full(w_s.shape), full(b_s_t.shape)] + slab_specs,
        out_specs=[pl.BlockSpec((None, MIX_TM, width), lambda b, i: (b, i, 0))] + slab_specs,
        out_shape=[jax.ShapeDtypeStruct((bsz, s, width), BF16)]
        + [jax.ShapeDtypeStruct(w.shape, BF16) for w in cast_weights],
        compiler_params=_params("arbitrary", "arbitrary"),
        name="gmlp",
    )(x, shift, scale, w_in, w_in, ln_g.reshape(1, width), ln_b.reshape(1, width),
      w_s, b_s_t, *cast_weights)


def _rope(x, cos, sin_signed, first_half):
    rot = jnp.where(first_half, pltpu.roll(x, 3 * LANES // 4, 1),
                    pltpu.roll(x, LANES // 4, 1))
    return x * cos + rot * sin_signed


def _qkv_kernel(x_ref, sh_ref, sc_ref, wq_ref, wk_ref, wv_ref, cos_ref, sin_ref,
                qq_ref, k_ref, vt_ref):
    t = x_ref.shape[0]
    h = (x_ref[...] * (1.0 + sc_ref[...]) + sh_ref[...]).astype(BF16)
    cos = cos_ref[...]
    sin = sin_ref[...]
    lane = lax.broadcasted_iota(jnp.int32, (t, LANES), 1)
    first_half = (lane % DIFF_QK_DIM) < (DIFF_QK_DIM // 2)
    comp = lax.broadcasted_iota(jnp.int32, (LANES, t), 0) < DIFF_QK_DIM
    q_scale = DIFF_QK_DIM ** -0.5 * LOG2_E
    ones = jnp.ones((V_AUG_ROWS - DIFF_V_DIM, t), BF16)
    row_chunk = lax.broadcasted_iota(jnp.int32, (t, LANES), 0) // CHUNK
    chunk_onehot = jnp.where(row_chunk == lane, 1.0, 0.0).astype(BF16)
    q = jnp.dot(h, wq_ref[...], preferred_element_type=F32)
    k = jnp.dot(h, wk_ref[...], preferred_element_type=F32)
    v = jnp.dot(h, wv_ref[...], preferred_element_type=F32)
    for hd in range(DIFF_HEADS):
        cols = slice(hd * LANES, (hd + 1) * LANES)
        k_ref[hd, :, :LANES] = _rope(k[:, cols], cos, sin, first_half).astype(BF16)
        k_ref[hd, :, LANES:] = chunk_onehot
        qt = (_rope(q[:, cols], cos, sin, first_half) * q_scale).T
        qq_ref[hd, 0, :, :t] = jnp.where(comp, qt, 0.0).astype(BF16)
        qq_ref[hd, 0, :, t:] = jnp.where(comp, 0.0, qt).astype(BF16)
        vt_ref[hd, 0, :DIFF_V_DIM, :] = v[:, cols].T.astype(BF16)
        vt_ref[hd, 0, DIFF_V_DIM:, :] = ones


def _qkv_rope(x, shift, scale, w_in, width, cos, sin_signed):
    bsz, s, d = x.shape
    t = ATT_T
    nt = s // t
    vec = pl.BlockSpec((None, 1, d), lambda b, i: (b, 0, 0))
    w_cols = lambda blk: pl.BlockSpec((d, width), lambda b, i: (0, blk))
    tab = pl.BlockSpec((t, LANES), lambda b, i: (i, 0))
    return pl.pallas_call(
        _qkv_kernel,
        grid=(bsz, nt),
        in_specs=[pl.BlockSpec((None, t, d), lambda b, i: (b, i, 0)), vec, vec,
                  w_cols(2), w_cols(3), w_cols(4), tab, tab],
        out_specs=[
            pl.BlockSpec((None, DIFF_HEADS, 1, LANES, 2 * t), lambda b, i: (b, 0, i, 0, 0)),
            pl.BlockSpec((None, DIFF_HEADS, t, 2 * LANES), lambda b, i: (b, 0, i, 0)),
            pl.BlockSpec((None, DIFF_HEADS, 1, V_AUG_ROWS, t), lambda b, i: (b, 0, i, 0, 0)),
        ],
        out_shape=[
            jax.ShapeDtypeStruct((bsz, DIFF_HEADS, nt, LANES, 2 * t), BF16),
            jax.ShapeDtypeStruct((bsz, DIFF_HEADS, s, 2 * LANES), BF16),
            jax.ShapeDtypeStruct((bsz, DIFF_HEADS, nt, V_AUG_ROWS, t), BF16),
        ],
        compiler_params=_params("arbitrary", "arbitrary"),
        name="qkv_rope",
    )(x, shift, scale, w_in, w_in, w_in, cos, sin_signed)


def _attn_kernel(lam_init, n_steps, sched_ref, qq_ref, k_ref, vt_ref, bias_ref,
                 lq1_ref, lk1_ref, lq2_ref, lk2_ref, sg_ref, c_ref, wada_ref, bada_ref,
                 o_ref, mod_ref,
                 m_ref, acc_ref, fin_ref, s0_ref, s1_ref, smax0_ref, smax1_ref):
    t = ATT_T
    mod_ref[...] = _adaln_block(c_ref, wada_ref, bada_ref)

    m_ref[...] = jnp.full_like(m_ref, NEG_INF)
    acc_ref[...] = jnp.zeros_like(acc_ref)
    buffers = ((s0_ref, smax0_ref), (s1_ref, smax1_ref))

    def produce(n, s_ref, smax_ref):
        qi, kj, dg = sched_ref[0, n], sched_ref[1, n], sched_ref[2, n]
        rhs = jnp.concatenate([qq_ref[qi], bias_ref[dg]], axis=0)
        kblk = k_ref[pl.ds(pl.multiple_of(kj * t, t), t), :]
        s = jnp.dot(kblk, rhs, preferred_element_type=F32)
        s_ref[...] = s
        smax_ref[...] = jnp.max(s, axis=0, keepdims=True)

    def consume(n, s_ref, smax_ref):
        qi, kj = sched_ref[0, n], sched_ref[1, n]
        s = s_ref[...]
        m_prev = jnp.where(kj == 0, NEG_INF, m_ref[...])
        m_new = jnp.maximum(m_prev, smax_ref[...])
        alpha = jnp.exp2(m_prev - m_new)
        p = jnp.exp2(s - m_new).astype(BF16)
        pv = jnp.dot(vt_ref[kj], p, preferred_element_type=F32)
        acc = alpha * acc_ref[...] + pv
        acc_ref[...] = acc
        fin_ref[qi] = acc
        m_ref[...] = m_new

    produce(0, *buffers[0])
    produce(1, *buffers[1])

    def trip(g, carry):
        for u in range(ATT_UNROLL):
            n = g * ATT_UNROLL + u
            consume(n, *buffers[u % 2])
            produce(n + 2, *buffers[u % 2])
        return carry

    lax.fori_loop(0, n_steps // ATT_UNROLL, trip, 0)

    lam = (jnp.exp(jnp.sum(lq1_ref[...] * lk1_ref[...], keepdims=True))
           - jnp.exp(jnp.sum(lq2_ref[...] * lk2_ref[...], keepdims=True))
           + lam_init)
    sub_g = sg_ref[...] * (1.0 - lam_init)

    for qi in range(fin_ref.shape[0]):
        acc = fin_ref[qi]
        o = acc[:DIFF_V_DIM, :] / acc[DIFF_V_DIM:DIFF_V_DIM + 1, :]
        o = o[:, :t] - lam * o[:, t:]
        ms = jnp.mean(o * o, axis=0, keepdims=True)
        o = o * lax.rsqrt(ms + LN_EPS) * sub_g
        o_ref[qi * t:(qi + 1) * t, :] = o.T.astype(BF16)


def _attn_schedule(nt):
    steps = [(qi, kj, int(kj == qi)) for qi in range(nt) for kj in range(qi + 1)]
    n_steps = len(steps)
    steps += [(0, 0, 1)] * 2
    return jnp.asarray(steps, jnp.int32).T, n_steps


def _attn_mask_bias(t):
    key_chunk = np.arange(LANES)[:, None]
    q_chunk = (np.arange(2 * t)[None, :] % t) // CHUNK
    hidden = (key_chunk < t // CHUNK) & (key_chunk > q_chunk)
    diag = np.where(hidden, NEG_INF, 0.0).astype(np.float32)
    return jnp.asarray(np.stack([np.zeros_like(diag), diag]), dtype=BF16)


def _diff_attn(qq, k, vt, lq1, lk1, lq2, lk2, sub_g, lam_init, c_pad, w_ada, b_ada, ada_col0):
    bsz, heads, nt, _, _ = qq.shape
    t = ATT_T
    s = nt * t
    sched, n_steps = _attn_schedule(nt)
    assert n_steps % ATT_UNROLL == 0 and ATT_UNROLL % 2 == 0
    n_tail = w_ada.shape[1] - ada_col0
    ada_tn = n_tail // (bsz * heads)
    assert ada_tn * bsz * heads == n_tail and ada_tn % LANES == 0 and ada_col0 % ada_tn == 0
    ada_blk0 = ada_col0 // ada_tn
    lam_spec = pl.BlockSpec((1, DIFF_QK_DIM), lambda b, h, sc: (0, 0))
    grid_spec = pltpu.PrefetchScalarGridSpec(
        num_scalar_prefetch=1,
        grid=(bsz, heads),
        in_specs=[
            pl.BlockSpec((None, None, nt, LANES, 2 * t), lambda b, h, sc: (b, h, 0, 0, 0)),
            pl.BlockSpec((None, None, s, 2 * LANES), lambda b, h, sc: (b, h, 0, 0)),
            pl.BlockSpec((None, None, nt, V_AUG_ROWS, t), lambda b, h, sc: (b, h, 0, 0, 0)),
            pl.BlockSpec((2, LANES, 2 * t), lambda b, h, sc: (0, 0, 0)),
            lam_spec, lam_spec, lam_spec, lam_spec,
            pl.BlockSpec((DIFF_V_DIM, 1), lambda b, h, sc: (0, 0)),
            pl.BlockSpec(c_pad.shape, lambda b, h, sc: (0, 0)),
            pl.BlockSpec((w_ada.shape[0], ada_tn),
                         lambda b, h, sc: (0, ada_blk0 + b * heads + h)),
            pl.BlockSpec((1, ada_tn), lambda b, h, sc: (0, ada_blk0 + b * heads + h)),
        ],
        out_specs=[pl.BlockSpec((None, s, LANES), lambda b, h, sc: (b, 0, h)),
                   pl.BlockSpec((SUBLANES, ada_tn), lambda b, h, sc: (0, b * heads + h))],
        scratch_shapes=[pltpu.VMEM((1, 2 * t), F32),
                        pltpu.VMEM((V_AUG_ROWS, 2 * t), F32),
                        pltpu.VMEM((nt, V_AUG_ROWS, 2 * t), F32),
                        pltpu.VMEM((t, 2 * t), F32), pltpu.VMEM((t, 2 * t), F32),
                        pltpu.VMEM((1, 2 * t), F32), pltpu.VMEM((1, 2 * t), F32)],
    )
    return pl.pallas_call(
        functools.partial(_attn_kernel, lam_init, n_steps),
        grid_spec=grid_spec,
        out_shape=[jax.ShapeDtypeStruct((bsz, s, heads * DIFF_V_DIM), BF16),
                   jax.ShapeDtypeStruct((SUBLANES, n_tail), F32)],
        compiler_params=_params("arbitrary", "arbitrary"),
        name="diff_attn",
    )(sched, qq, k, vt, _attn_mask_bias(t), lq1.reshape(1, -1), lk1.reshape(1, -1),
      lq2.reshape(1, -1), lk2.reshape(1, -1), sub_g.reshape(-1, 1),
      c_pad, w_ada, b_ada.reshape(1, -1))


def _out_kernel(alpha, x_ref, gt_ref, a_ref, b_ref, wa_ref, wb_ref, lg_ref, lb_ref,
                o_ref):
    gate = 1.0 + gt_ref[...]
    for c in range(x_ref.shape[0] // ROW_CHUNK):
        rows = pl.ds(c * ROW_CHUNK, ROW_CHUNK)
        y = jnp.dot(a_ref[rows, :], wa_ref[...], preferred_element_type=F32)
        y = y + jnp.dot(b_ref[rows, :], wb_ref[...], preferred_element_type=F32)
        o_ref[rows, :] = _residual_layer_norm(alpha, x_ref[rows, :], gate, y,
                                              lg_ref[...], lb_ref[...])


def _out_ln(x, gate, out_a, out_b, w_out, ln_g, ln_b, alpha):
    bsz, s, d = x.shape
    half = out_a.shape[-1]
    tile = pl.BlockSpec((None, MIX_TM, d), lambda b, i: (b, i, 0))
    act = pl.BlockSpec((None, MIX_TM, half), lambda b, i: (b, i, 0))
    row = pl.BlockSpec((1, d), lambda b, i: (0, 0))
    return pl.pallas_call(
        functools.partial(_out_kernel, alpha),
        grid=(bsz, s // MIX_TM),
        in_specs=[tile, pl.BlockSpec((None, 1, d), lambda b, i: (b, 0, 0)), act, act,
                  pl.BlockSpec((half, d), lambda b, i: (0, 0)),
                  pl.BlockSpec((half, d), lambda b, i: (1, 0)),
                  row, row],
        out_specs=tile,
        out_shape=jax.ShapeDtypeStruct((bsz, s, d), F32),
        compiler_params=_params("arbitrary", "arbitrary"),
        name="out_ln",
    )(x, gate, out_a, out_b, w_out, w_out, ln_g.reshape(1, d), ln_b.reshape(1, d))


def _rope_tables(s):
    half = DIFF_QK_DIM // 2
    pos = np.arange(s, dtype=np.float64)
    inv_freq = ROPE_THETA ** (-np.arange(0, DIFF_QK_DIM, 2, dtype=np.float64) / DIFF_QK_DIM)
    ang = pos[:, None] * inv_freq[None, :]
    cos, sin = np.cos(ang), np.sin(ang)
    cos = np.concatenate([cos, cos, cos, cos], axis=-1)
    sin_signed = np.concatenate([-sin, sin, -sin, sin], axis=-1)
    assert cos.shape[-1] == 4 * half == LANES
    return jnp.asarray(cos, dtype=F32), jnp.asarray(sin_signed, dtype=F32)


def kernel(x, c, w_ada, b_ada, ffn1_w_gu, ffn1_w_down, ln1_g, ln1_b, w_in, gmlp_ln_g, gmlp_ln_b, gmlp_w_s, gmlp_b_s, lambda_q1, lambda_k1, lambda_q2, lambda_k2, diff_subln_g, w_out, ln2_g, ln2_b, ffn2_w_gu, ffn2_w_down, ln3_g, ln3_b):
    bsz, s, d = x.shape
    depth = w_ada.shape[0]
    alpha = (2 * depth) ** 0.25
    gw = gmlp_ln_g.shape[-1]
    qk = DIFF_HEADS * 2 * DIFF_QK_DIM
    cos, sin_signed = _rope_tables(s)
    assert bsz <= SUBLANES
    c_pad = jnp.zeros((SUBLANES, d), F32).at[:bsz].set(c)

    assert qk == gw and w_in.shape[-1] == 5 * gw
    n_early = 5

    for l in range(depth):
        mod_early = _adaln(c_pad, w_ada[l], b_ada[l], n_early * d)[:bsz]
        s1, sc1, g1, s2, sc2 = [m[:, None, :] for m in jnp.split(mod_early, n_early, axis=-1)]

        x, w_in_b = _ffn_ln_f32_weights(x, s1, sc1, g1, ffn1_w_gu[l], ffn1_w_down[l],
                                        ln1_g[l], ln1_b[l], alpha, side_cast=w_in[l])
        out_a, w_out_b, w_gu2_b, w_down2_b = _gmlp(
            x, s2, sc2, w_in_b, gw, gmlp_ln_g[l], gmlp_ln_b[l],
            gmlp_w_s[l], jnp.transpose(gmlp_b_s[l]),
            cast_weights=(w_out[l], ffn2_w_gu[l], ffn2_w_down[l]))
        qq, k, vt = _qkv_rope(x, s2, sc2, w_in_b, gw, cos, sin_signed)
        lam_init = 0.8 - 0.6 * math.exp(-0.3 * l)
        out_b, mod_late = _diff_attn(qq, k, vt, lambda_q1[l], lambda_k1[l], lambda_q2[l],
                                     lambda_k2[l], diff_subln_g[l], lam_init,
                                     c_pad, w_ada[l], b_ada[l], n_early * d)
        g2, s3, sc3, g3 = [m[:, None, :]
                           for m in jnp.split(mod_late[:bsz], N_MOD - n_early, axis=-1)]
        x = _out_ln(x, g2, out_a, out_b, w_out_b, ln2_g[l], ln2_b[l], alpha)

        x = _ffn_ln(x, s3, sc3, g3, w_gu2_b, w_down2_b, ln3_g[l], ln3_b[l], alpha)
    return x
```

```python
import functools
import math

import jax
import jax.numpy as jnp
import numpy as np
from jax import lax
from jax.experimental import pallas as pl
from jax.experimental.pallas import tpu as pltpu

F32 = jnp.float32
BF16 = jnp.bfloat16

CHUNK = 64
GMLP_BLOCK = 128
GMLP_GROUPS = 8
DIFF_HEADS = 8
DIFF_V_DIM = 128
DIFF_QK_DIM = 64
N_MOD = 9
ROPE_THETA = 10000.0
LN_EPS = 1e-5
NEG_INF = -1e30
LOG2_E = 1.4426950408889634

SUBLANES = 8
BF16_SUBLANE_ROWS = 2 * SUBLANES
LANES = 128
VMEM_LIMIT_BYTES = 56 * 1024 * 1024
V_AUG_ROWS = DIFF_V_DIM + BF16_SUBLANE_ROWS
FFN_VMEM_LIMIT_BYTES = 61 * 1024 * 1024

ADA_TN = 1024
FFN_TM = 1024
FFN_TF = 512
FFN_TF_F32 = 256
FFN_UP_COLS = 256
FFN_DOWN_COLS = 256
CAST_BLOCK = (128, 512)
MIX_TM = 512
ROW_CHUNK = 256
PROJ_COLS = 256
ATT_T = 512
ATT_UNROLL = 34


def _params(*sem, vmem_limit_bytes=VMEM_LIMIT_BYTES):
    return pltpu.CompilerParams(dimension_semantics=sem,
                                vmem_limit_bytes=vmem_limit_bytes)


def _layer_norm_rows(z, g, b, eps=LN_EPS):
    mu = jnp.mean(z, axis=-1, keepdims=True)
    zc = z - mu
    var = jnp.mean(zc * zc, axis=-1, keepdims=True)
    return zc * lax.rsqrt(var + eps) * g + b


def _residual_layer_norm(alpha, x, gate, y, g, b):
    return _layer_norm_rows(x + (gate * (1.0 / alpha)) * y, g, b, eps=LN_EPS / alpha ** 2)


def _adaln_block(c_ref, w_ref, b_ref):
    c = c_ref[...]
    ca = (c / (1.0 + jnp.exp(-c))).astype(BF16)
    return jnp.dot(ca, w_ref[...].astype(BF16), preferred_element_type=F32) + b_ref[...]


def _adaln_kernel(c_ref, w_ref, b_ref, o_ref):
    o_ref[...] = _adaln_block(c_ref, w_ref, b_ref)


def _adaln(c_pad, w, b, n_cols):
    d = w.shape[0]
    return pl.pallas_call(
        _adaln_kernel,
        grid=(n_cols // ADA_TN,),
        in_specs=[pl.BlockSpec(c_pad.shape, lambda j: (0, 0)),
                  pl.BlockSpec((d, ADA_TN), lambda j: (0, j)),
                  pl.BlockSpec((1, ADA_TN), lambda j: (0, j))],
        out_specs=pl.BlockSpec((SUBLANES, ADA_TN), lambda j: (0, j)),
        out_shape=jax.ShapeDtypeStruct((SUBLANES, n_cols), F32),
        compiler_params=_params("arbitrary"),
        name="adaln",
    )(c_pad, w, b.reshape(1, -1))


def _ffn_kernel(alpha, side_cast, own_cast, aliased, x_ref, sh_ref, sc_ref, gt_ref,
                wg_ref, wu_ref, wd_ref, lg_ref, lb_ref, *rest):
    rest = list(rest)
    cast_in = rest.pop(0) if side_cast else None
    if aliased:
        rest.pop(0)
    o_ref = rest.pop(0)
    casts = []
    if side_cast:
        casts.append((cast_in, rest.pop(0)))
    if own_cast:
        copies = rest[:3]
        rest = rest[3:]
        casts += list(zip((wg_ref, wu_ref, wd_ref), copies))
        wg_ref, wu_ref, wd_ref = copies
    h_ref, a_ref = rest
    j = pl.program_id(1)
    last = pl.num_programs(1) - 1

    def run_casts():
        for src, dst in casts:
            dst[...] = src[...].astype(BF16)

    def hidden():
        h = h_ref[...]
        for c in range(a_ref.shape[1] // FFN_UP_COLS):
            cols = slice(c * FFN_UP_COLS, (c + 1) * FFN_UP_COLS)
            g = jnp.dot(h, wg_ref[:, cols], preferred_element_type=F32)
            u = jnp.dot(h, wu_ref[:, cols], preferred_element_type=F32)
            a_ref[:, cols] = (g / (1.0 + jnp.exp(-g)) * u).astype(BF16)

    def down(first):
        a = a_ref[...]
        for c in range(o_ref.shape[1] // FFN_DOWN_COLS):
            cols = slice(c * FFN_DOWN_COLS, (c + 1) * FFN_DOWN_COLS)
            y = jnp.dot(a, wd_ref[:, cols], preferred_element_type=F32)
            o_ref[:, cols] = y if first else o_ref[:, cols] + y

    @pl.when(j == 0)
    def _():
        run_casts()
        h_ref[...] = (x_ref[...] * (1.0 + sc_ref[...]) + sh_ref[...]).astype(BF16)
        hidden()
        down(first=True)

    @pl.when(jnp.logical_and(j > 0, j < last))
    def _():
        run_casts()
        hidden()
        down(first=False)

    @pl.when(j == last)
    def _():
        run_casts()
        hidden()
        gate = 0.5 * (1.0 + gt_ref[...])
        for r in range(o_ref.shape[0] // ROW_CHUNK):
            rows = pl.ds(r * ROW_CHUNK, ROW_CHUNK)
            y = o_ref[rows, :] + jnp.dot(a_ref[rows, :], wd_ref[...],
                                         preferred_element_type=F32)
            o_ref[rows, :] = _residual_layer_norm(alpha, x_ref[rows, :], gate, y,
                                                  lg_ref[...], lb_ref[...])


def _ffn_call(x, shift, scale, gate, w_gate, gate_blk0, w_up, up_blk0, w_down, ln_g, ln_b,
              alpha, *, tf, tile0, n_tiles, side_cast=None, own_cast=False, prev_out=None):
    bsz, s, d = x.shape
    f = w_down.shape[0]
    nf = f // tf
    nti = s // FFN_TM

    def token_tile(t):
        return (t + tile0) // nti, (t + tile0) % nti

    vec = pl.BlockSpec((None, 1, d), lambda t, j: (token_tile(t)[0], 0, 0))
    row = pl.BlockSpec((1, d), lambda t, j: (0, 0))
    tile = pl.BlockSpec((None, FFN_TM, d), lambda t, j: (*token_tile(t), 0))
    in_specs = [tile, vec, vec, vec,
                pl.BlockSpec((d, tf), lambda t, j: (0, gate_blk0 + j)),
                pl.BlockSpec((d, tf), lambda t, j: (0, up_blk0 + j)),
                pl.BlockSpec((tf, d), lambda t, j: (j, 0)),
                row, row]
    out_specs, out_shape = [tile], [jax.ShapeDtypeStruct((bsz, s, d), F32)]
    operands = [x, shift, scale, gate, w_gate, w_up, w_down,
                ln_g.reshape(1, d), ln_b.reshape(1, d)]
    aliases = {}
    if side_cast is not None:
        nrb, ncb = side_cast.shape[0] // CAST_BLOCK[0], side_cast.shape[1] // CAST_BLOCK[1]
        assert (nrb * CAST_BLOCK[0], ncb * CAST_BLOCK[1]) == side_cast.shape
        assert nrb * ncb <= n_tiles * nf

        def cast_index(t, j):
            blk = jnp.minimum(t * nf + j, nrb * ncb - 1)
            return blk // ncb, blk % ncb

        cast_spec = pl.BlockSpec(CAST_BLOCK, cast_index)
        in_specs.append(cast_spec)
        out_specs.append(cast_spec)
        out_shape.append(jax.ShapeDtypeStruct(side_cast.shape, BF16))
        operands.append(side_cast)
    if prev_out is not None:
        in_specs.append(pl.BlockSpec(memory_space=pl.ANY))
        operands.append(prev_out)
        aliases = {len(operands) - 1: 0}
    if own_cast:
        out_specs += [pl.BlockSpec((d, tf), lambda t, j: (0, j)),
                      pl.BlockSpec((d, tf), lambda t, j: (0, j)),
                      pl.BlockSpec((tf, d), lambda t, j: (j, 0))]
        out_shape += [jax.ShapeDtypeStruct((d, f), BF16), jax.ShapeDtypeStruct((d, f), BF16),
                      jax.ShapeDtypeStruct((f, d), BF16)]
    return pl.pallas_call(
        functools.partial(_ffn_kernel, alpha, side_cast is not None, own_cast,
                          prev_out is not None),
        grid=(n_tiles, nf),
        in_specs=in_specs,
        out_specs=out_specs,
        out_shape=out_shape,
        input_output_aliases=aliases,
        scratch_shapes=[pltpu.VMEM((FFN_TM, d), BF16), pltpu.VMEM((FFN_TM, tf), BF16)],
        compiler_params=_params("arbitrary", "arbitrary",
                                vmem_limit_bytes=FFN_VMEM_LIMIT_BYTES),
        name="ffn_ln",
    )(*operands)


def _ffn_ln(x, shift, scale, gate, w_gu, w_down, ln_g, ln_b, alpha):
    bsz, s, _ = x.shape
    nf = w_down.shape[0] // FFN_TF
    return _ffn_call(x, shift, scale, gate, w_gu, 0, w_gu, nf, w_down, ln_g, ln_b, alpha,
                     tf=FFN_TF, tile0=0, n_tiles=bsz * (s // FFN_TM))[0]


def _ffn_ln_f32_weights(x, shift, scale, gate, w_gu, w_down, ln_g, ln_b, alpha, side_cast):
    bsz, s, _ = x.shape
    n_tiles = bsz * (s // FFN_TM)
    nf_first = w_down.shape[0] // FFN_TF_F32
    x_first, w_gate_b, w_up_b, w_down_b = _ffn_call(
        x, shift, scale, gate, w_gu, 0, w_gu, nf_first, w_down, ln_g, ln_b, alpha,
        tf=FFN_TF_F32, tile0=0, n_tiles=1, own_cast=True)
    x_out, side_b = _ffn_call(
        x, shift, scale, gate, w_gate_b, 0, w_up_b, 0, w_down_b, ln_g, ln_b, alpha,
        tf=FFN_TF, tile0=1, n_tiles=n_tiles - 1, side_cast=side_cast, prev_out=x_first)
    return x_out, side_b


def _gmlp_kernel(n_cast, x_ref, sh_ref, sc_ref, wu_ref, wv_ref, lg_ref, lb_ref, ws_ref,
                 bst_ref, *rest):
    cast_in, o_ref, cast_out = rest[:n_cast], rest[n_cast], rest[n_cast + 1:]
    for src, dst in zip(cast_in, cast_out):
        dst[...] = src[...].astype(BF16)

    pi = lax.broadcasted_iota(jnp.int32, (GMLP_BLOCK, GMLP_BLOCK), 0) // CHUNK
    pj = lax.broadcasted_iota(jnp.int32, (GMLP_BLOCK, GMLP_BLOCK), 1) // CHUNK
    keep = pj <= pi
    bst = bst_ref[...]

    h = (x_ref[...] * (1.0 + sc_ref[...]) + sh_ref[...]).astype(BF16)

    def gelu_proj(w_ref):
        parts = []
        for c in range(w_ref.shape[1] // PROJ_COLS):
            cols = slice(c * PROJ_COLS, (c + 1) * PROJ_COLS)
            parts.append(jax.nn.gelu(jnp.dot(h, w_ref[:, cols], preferred_element_type=F32)))
        return jnp.concatenate(parts, axis=1)

    v = _layer_norm_rows(gelu_proj(wv_ref), lg_ref[...], lb_ref[...]).astype(BF16)
    u = gelu_proj(wu_ref)
    n_blk = x_ref.shape[0] // GMLP_BLOCK
    for g in range(GMLP_GROUPS):
        w = jnp.where(keep, ws_ref[g], 0.0).astype(BF16)
        bias = bst[:, g:g + 1]
        cols = slice(g * LANES, (g + 1) * LANES)
        v_blocks = jnp.concatenate(
            [v[blk * GMLP_BLOCK:(blk + 1) * GMLP_BLOCK, cols] for blk in range(n_blk)], axis=1)
        mixed = jnp.dot(w, v_blocks, preferred_element_type=F32) + bias
        for blk in range(n_blk):
            rows = slice(blk * GMLP_BLOCK, (blk + 1) * GMLP_BLOCK)
            o_ref[rows, cols] = (u[rows, cols]
                                 * mixed[:, blk * LANES:(blk + 1) * LANES]).astype(BF16)


def _gmlp(x, shift, scale, w_in, width, ln_g, ln_b, w_s, b_s_t, cast_weights):
    bsz, s, d = x.shape
    nt = s // MIX_TM
    vec = pl.BlockSpec((None, 1, d), lambda b, i: (b, 0, 0))
    full = lambda shape: pl.BlockSpec(shape, lambda b, i: (0,) * len(shape))
    w_cols = lambda blk: pl.BlockSpec((d, width), lambda b, i: (0, blk))
    slab_specs = []
    for w in cast_weights:
        rows = w.shape[0] // (bsz * nt)
        assert rows * bsz * nt == w.shape[0] and rows % BF16_SUBLANE_ROWS == 0, w.shape
        slab_specs.append(pl.BlockSpec((rows, w.shape[1]), lambda b, i: (b * nt + i, 0)))
    return pl.pallas_call(
        functools.partial(_gmlp_kernel, len(cast_weights)),
        grid=(bsz, nt),
        in_specs=[pl.BlockSpec((None, MIX_TM, d), lambda b, i: (b, i, 0)), vec, vec,
                  w_cols(0), w_cols(1),
                  full((1, width)), full((1, width)),
                  full(w_s.shape), full(b_s_t.shape)] + slab_specs,
        out_specs=[pl.BlockSpec((None, MIX_TM, width), lambda b, i: (b, i, 0))] + slab_specs,
        out_shape=[jax.ShapeDtypeStruct((bsz, s, width), BF16)]
        + [jax.ShapeDtypeStruct(w.shape, BF16) for w in cast_weights],
        compiler_params=_params("arbitrary", "arbitrary"),
        name="gmlp",
    )(x, shift, scale, w_in, w_in, ln_g.reshape(1, width), ln_b.reshape(1, width),
      w_s, b_s_t, *cast_weights)


def _rope(x, cos, sin_signed, first_half):
    rot = jnp.where(first_half, pltpu.roll(x, 3 * LANES // 4, 1),
                    pltpu.roll(x, LANES // 4, 1))
    return x * cos + rot * sin_signed


def _qkv_kernel(x_ref, sh_ref, sc_ref, wq_ref, wk_ref, wv_ref, cos_ref, sin_ref,
                qq_ref, k_ref, vt_ref):
    t = x_ref.shape[0]
    h = (x_ref[...] * (1.0 + sc_ref[...]) + sh_ref[...]).astype(BF16)
    cos = cos_ref[...]
    sin = sin_ref[...]
    lane = lax.broadcasted_iota(jnp.int32, (t, LANES), 1)
    first_half = (lane % DIFF_QK_DIM) < (DIFF_QK_DIM // 2)
    comp = lax.broadcasted_iota(jnp.int32, (LANES, t), 0) < DIFF_QK_DIM
    q_scale = DIFF_QK_DIM ** -0.5 * LOG2_E
    ones = jnp.ones((V_AUG_ROWS - DIFF_V_DIM, t), BF16)
    q = jnp.dot(h, wq_ref[...], preferred_element_type=F32)
    k = jnp.dot(h, wk_ref[...], preferred_element_type=F32)
    v = jnp.dot(h, wv_ref[...], preferred_element_type=F32)
    for hd in range(DIFF_HEADS):
        cols = slice(hd * LANES, (hd + 1) * LANES)
        k_ref[hd] = _rope(k[:, cols], cos, sin, first_half).astype(BF16)
        qt = (_rope(q[:, cols], cos, sin, first_half) * q_scale).T
        qq_ref[hd, 0, :, :t] = jnp.where(comp, qt, 0.0).astype(BF16)
        qq_ref[hd, 0, :, t:] = jnp.where(comp, 0.0, qt).astype(BF16)
        vt_ref[hd, 0, :DIFF_V_DIM, :] = v[:, cols].T.astype(BF16)
        vt_ref[hd, 0, DIFF_V_DIM:, :] = ones


def _qkv_rope(x, shift, scale, w_in, width, cos, sin_signed):
    bsz, s, d = x.shape
    t = ATT_T
    nt = s // t
    vec = pl.BlockSpec((None, 1, d), lambda b, i: (b, 0, 0))
    w_cols = lambda blk: pl.BlockSpec((d, width), lambda b, i: (0, blk))
    tab = pl.BlockSpec((t, LANES), lambda b, i: (i, 0))
    return pl.pallas_call(
        _qkv_kernel,
        grid=(bsz, nt),
        in_specs=[pl.BlockSpec((None, t, d), lambda b, i: (b, i, 0)), vec, vec,
                  w_cols(2), w_cols(3), w_cols(4), tab, tab],
        out_specs=[
            pl.BlockSpec((None, DIFF_HEADS, 1, LANES, 2 * t), lambda b, i: (b, 0, i, 0, 0)),
            pl.BlockSpec((None, DIFF_HEADS, t, LANES), lambda b, i: (b, 0, i, 0)),
            pl.BlockSpec((None, DIFF_HEADS, 1, V_AUG_ROWS, t), lambda b, i: (b, 0, i, 0, 0)),
        ],
        out_shape=[
            jax.ShapeDtypeStruct((bsz, DIFF_HEADS, nt, LANES, 2 * t), BF16),
            jax.ShapeDtypeStruct((bsz, DIFF_HEADS, s, LANES), BF16),
            jax.ShapeDtypeStruct((bsz, DIFF_HEADS, nt, V_AUG_ROWS, t), BF16),
        ],
        compiler_params=_params("arbitrary", "arbitrary"),
        name="qkv_rope",
    )(x, shift, scale, w_in, w_in, w_in, cos, sin_signed)


def _attn_kernel(lam_init, n_steps, sched_ref, qq_ref, k_ref, vt_ref, bias_ref, onehot_ref,
                 lq1_ref, lk1_ref, lq2_ref, lk2_ref, sg_ref, c_ref, wada_ref, bada_ref,
                 o_ref, mod_ref,
                 m_ref, acc_ref, fin_ref, s0_ref, s1_ref, smax0_ref, smax1_ref):
    t = ATT_T
    mod_ref[...] = _adaln_block(c_ref, wada_ref, bada_ref)

    m_ref[...] = jnp.full_like(m_ref, NEG_INF)
    acc_ref[...] = jnp.zeros_like(acc_ref)
    buffers = ((s0_ref, smax0_ref), (s1_ref, smax1_ref))

    def produce(n, s_ref, smax_ref):
        qi, kj, dg = sched_ref[0, n], sched_ref[1, n], sched_ref[2, n]
        rhs = jnp.concatenate([qq_ref[qi], bias_ref[dg]], axis=0)
        kblk = jnp.concatenate([k_ref[pl.ds(pl.multiple_of(kj * t, t), t), :],
                                onehot_ref[...]], axis=1)
        s = jnp.dot(kblk, rhs, preferred_element_type=F32)
        s_ref[...] = s
        smax_ref[...] = jnp.max(s, axis=0, keepdims=True)

    def consume(n, s_ref, smax_ref):
        qi, kj = sched_ref[0, n], sched_ref[1, n]
        s = s_ref[...]
        m_prev = jnp.where(kj == 0, NEG_INF, m_ref[...])
        m_new = jnp.maximum(m_prev, smax_ref[...])
        alpha = jnp.exp2(m_prev - m_new)
        p = jnp.exp2(s - m_new).astype(BF16)
        pv = jnp.dot(vt_ref[kj], p, preferred_element_type=F32)
        acc = alpha * acc_ref[...] + pv
        acc_ref[...] = acc
        fin_ref[qi] = acc
        m_ref[...] = m_new

    produce(0, *buffers[0])
    produce(1, *buffers[1])

    def trip(g, carry):
        for u in range(ATT_UNROLL):
            n = g * ATT_UNROLL + u
            consume(n, *buffers[u % 2])
            produce(n + 2, *buffers[u % 2])
        return carry

    lax.fori_loop(0, n_steps // ATT_UNROLL, trip, 0)

    lam = (jnp.exp(jnp.sum(lq1_ref[...] * lk1_ref[...], keepdims=True))
           - jnp.exp(jnp.sum(lq2_ref[...] * lk2_ref[...], keepdims=True))
           + lam_init)
    sub_g = sg_ref[...] * (1.0 - lam_init)

    for qi in range(fin_ref.shape[0]):
        acc = fin_ref[qi]
        o = acc[:DIFF_V_DIM, :] / acc[DIFF_V_DIM:DIFF_V_DIM + 1, :]
        o = o[:, :t] - lam * o[:, t:]
        ms = jnp.mean(o * o, axis=0, keepdims=True)
        o = o * lax.rsqrt(ms + LN_EPS) * sub_g
        o_ref[qi * t:(qi + 1) * t, :] = o.T.astype(BF16)


def _attn_schedule(nt):
    steps = [(qi, kj, int(kj == qi)) for qi in range(nt) for kj in range(qi + 1)]
    n_steps = len(steps)
    steps += [(0, 0, 1)] * 2
    return jnp.asarray(steps, jnp.int32).T, n_steps


def _chunk_onehot(t):
    onehot = (np.arange(t)[:, None] // CHUNK) == np.arange(LANES)[None, :]
    return jnp.asarray(onehot.astype(np.float32), dtype=BF16)


def _attn_mask_bias(t):
    key_chunk = np.arange(LANES)[:, None]
    q_chunk = (np.arange(2 * t)[None, :] % t) // CHUNK
    hidden = (key_chunk < t // CHUNK) & (key_chunk > q_chunk)
    diag = np.where(hidden, NEG_INF, 0.0).astype(np.float32)
    return jnp.asarray(np.stack([np.zeros_like(diag), diag]), dtype=BF16)


def _diff_attn(qq, k, vt, lq1, lk1, lq2, lk2, sub_g, lam_init, c_pad, w_ada, b_ada, ada_col0):
    bsz, heads, nt, _, _ = qq.shape
    t = ATT_T
    s = nt * t
    sched, n_steps = _attn_schedule(nt)
    assert n_steps % ATT_UNROLL == 0 and ATT_UNROLL % 2 == 0
    n_tail = w_ada.shape[1] - ada_col0
    ada_tn = n_tail // (bsz * heads)
    assert ada_tn * bsz * heads == n_tail and ada_tn % LANES == 0 and ada_col0 % ada_tn == 0
    ada_blk0 = ada_col0 // ada_tn
    lam_spec = pl.BlockSpec((1, DIFF_QK_DIM), lambda b, h, sc: (0, 0))
    grid_spec = pltpu.PrefetchScalarGridSpec(
        num_scalar_prefetch=1,
        grid=(bsz, heads),
        in_specs=[
            pl.BlockSpec((None, None, nt, LANES, 2 * t), lambda b, h, sc: (b, h, 0, 0, 0)),
            pl.BlockSpec((None, None, s, LANES), lambda b, h, sc: (b, h, 0, 0)),
            pl.BlockSpec((None, None, nt, V_AUG_ROWS, t), lambda b, h, sc: (b, h, 0, 0, 0)),
            pl.BlockSpec((2, LANES, 2 * t), lambda b, h, sc: (0, 0, 0)),
            pl.BlockSpec((t, LANES), lambda b, h, sc: (0, 0)),
            lam_spec, lam_spec, lam_spec, lam_spec,
            pl.BlockSpec((DIFF_V_DIM, 1), lambda b, h, sc: (0, 0)),
            pl.BlockSpec(c_pad.shape, lambda b, h, sc: (0, 0)),
            pl.BlockSpec((w_ada.shape[0], ada_tn),
                         lambda b, h, sc: (0, ada_blk0 + b * heads + h)),
            pl.BlockSpec((1, ada_tn), lambda b, h, sc: (0, ada_blk0 + b * heads + h)),
        ],
        out_specs=[pl.BlockSpec((None, s, LANES), lambda b, h, sc: (b, 0, h)),
                   pl.BlockSpec((SUBLANES, ada_tn), lambda b, h, sc: (0, b * heads + h))],
        scratch_shapes=[pltpu.VMEM((1, 2 * t), F32),
                        pltpu.VMEM((V_AUG_ROWS, 2 * t), F32),
                        pltpu.VMEM((nt, V_AUG_ROWS, 2 * t), F32),
                        pltpu.VMEM((t, 2 * t), F32), pltpu.VMEM((t, 2 * t), F32),
                        pltpu.VMEM((1, 2 * t), F32), pltpu.VMEM((1, 2 * t), F32)],
    )
    return pl.pallas_call(
        functools.partial(_attn_kernel, lam_init, n_steps),
        grid_spec=grid_spec,
        out_shape=[jax.ShapeDtypeStruct((bsz, s, heads * DIFF_V_DIM), BF16),
                   jax.ShapeDtypeStruct((SUBLANES, n_tail), F32)],
        compiler_params=_params("arbitrary", "arbitrary"),
        name="diff_attn",
    )(sched, qq, k, vt, _attn_mask_bias(t), _chunk_onehot(t),
      lq1.reshape(1, -1), lk1.reshape(1, -1),
      lq2.reshape(1, -1), lk2.reshape(1, -1), sub_g.reshape(-1, 1),
      c_pad, w_ada, b_ada.reshape(1, -1))


def _out_kernel(alpha, x_ref, gt_ref, a_ref, b_ref, wa_ref, wb_ref, lg_ref, lb_ref,
                o_ref):
    gate = 1.0 + gt_ref[...]
    for c in range(x_ref.shape[0] // ROW_CHUNK):
        rows = pl.ds(c * ROW_CHUNK, ROW_CHUNK)
        y = jnp.dot(a_ref[rows, :], wa_ref[...], preferred_element_type=F32)
        y = y + jnp.dot(b_ref[rows, :], wb_ref[...], preferred_element_type=F32)
        o_ref[rows, :] = _residual_layer_norm(alpha, x_ref[rows, :], gate, y,
                                              lg_ref[...], lb_ref[...])


def _out_ln(x, gate, out_a, out_b, w_out, ln_g, ln_b, alpha):
    bsz, s, d = x.shape
    half = out_a.shape[-1]
    tile = pl.BlockSpec((None, MIX_TM, d), lambda b, i: (b, i, 0))
    act = pl.BlockSpec((None, MIX_TM, half), lambda b, i: (b, i, 0))
    row = pl.BlockSpec((1, d), lambda b, i: (0, 0))
    return pl.pallas_call(
        functools.partial(_out_kernel, alpha),
        grid=(bsz, s // MIX_TM),
        in_specs=[tile, pl.BlockSpec((None, 1, d), lambda b, i: (b, 0, 0)), act, act,
                  pl.BlockSpec((half, d), lambda b, i: (0, 0)),
                  pl.BlockSpec((half, d), lambda b, i: (1, 0)),
                  row, row],
        out_specs=tile,
        out_shape=jax.ShapeDtypeStruct((bsz, s, d), F32),
        compiler_params=_params("arbitrary", "arbitrary"),
        name="out_ln",
    )(x, gate, out_a, out_b, w_out, w_out, ln_g.reshape(1, d), ln_b.reshape(1, d))


def _rope_tables(s):
    half = DIFF_QK_DIM // 2
    pos = np.arange(s, dtype=np.float64)
    inv_freq = ROPE_THETA ** (-np.arange(0, DIFF_QK_DIM, 2, dtype=np.float64) / DIFF_QK_DIM)
    ang = pos[:, None] * inv_freq[None, :]
    cos, sin = np.cos(ang), np.sin(ang)
    cos = np.concatenate([cos, cos, cos, cos], axis=-1)
    sin_signed = np.concatenate([-sin, sin, -sin, sin], axis=-1)
    assert cos.shape[-1] == 4 * half == LANES
    return jnp.asarray(cos, dtype=F32), jnp.asarray(sin_signed, dtype=F32)


def kernel(x, c, w_ada, b_ada, ffn1_w_gu, ffn1_w_down, ln1_g, ln1_b, w_in, gmlp_ln_g, gmlp_ln_b, gmlp_w_s, gmlp_b_s, lambda_q1, lambda_k1, lambda_q2, lambda_k2, diff_subln_g, w_out, ln2_g, ln2_b, ffn2_w_gu, ffn2_w_down, ln3_g, ln3_b):
    bsz, s, d = x.shape
    depth = w_ada.shape[0]
    alpha = (2 * depth) ** 0.25
    gw = gmlp_ln_g.shape[-1]
    qk = DIFF_HEADS * 2 * DIFF_QK_DIM
    cos, sin_signed = _rope_tables(s)
    assert bsz <= SUBLANES
    c_pad = jnp.zeros((SUBLANES, d), F32).at[:bsz].set(c)

    assert qk == gw and w_in.shape[-1] == 5 * gw
    n_early = 5

    for l in range(depth):
        mod_early = _adaln(c_pad, w_ada[l], b_ada[l], n_early * d)[:bsz]
        s1, sc1, g1, s2, sc2 = [m[:, None, :] for m in jnp.split(mod_early, n_early, axis=-1)]

        x, w_in_b = _ffn_ln_f32_weights(x, s1, sc1, g1, ffn1_w_gu[l], ffn1_w_down[l],
                                        ln1_g[l], ln1_b[l], alpha, side_cast=w_in[l])
        out_a, w_out_b, w_gu2_b, w_down2_b = _gmlp(
            x, s2, sc2, w_in_b, gw, gmlp_ln_g[l], gmlp_ln_b[l],
            gmlp_w_s[l], jnp.transpose(gmlp_b_s[l]),
            cast_weights=(w_out[l], ffn2_w_gu[l], ffn2_w_down[l]))
        qq, k, vt = _qkv_rope(x, s2, sc2, w_in_b, gw, cos, sin_signed)
        lam_init = 0.8 - 0.6 * math.exp(-0.3 * l)
        out_b, mod_late = _diff_attn(qq, k, vt, lambda_q1[l], lambda_k1[l], lambda_q2[l],
                                     lambda_k2[l], diff_subln_g[l], lam_init,
                                     c_pad, w_ada[l], b_ada[l], n_early * d)
        g2, s3, sc3, g3 = [m[:, None, :]
                           for m in jnp.split(mod_late[:bsz], N_MOD - n_early, axis=-1)]
        x = _out_ln(x, g2, out_a, out_b, w_out_b, ln2_g[l], ln2_b[l], alpha)

        x = _ffn_ln(x, s3, sc3, g3, w_gu2_b, w_down2_b, ln3_g[l], ln3_b[l], alpha)
    return x
```
